```python
import jax, jax.numpy as jnp
from jax import lax
import numpy as np

D_MODEL = 1024
BATCH = 8
SEQ = 2048
DEPTH = 1
DEC_BATCH = 128
DEC_SEQ = 1
PAST_LEN = 16384
PAGE_SIZE = 128

POOL_WINDOWS = (2, 4, 8, 16)
N_POOL_GROUPS = len(POOL_WINDOWS)
POOL_WIDTH = D_MODEL // 2
POOL_GROUP = POOL_WIDTH // N_POOL_GROUPS
POOL_STATE = max(POOL_WINDOWS) - 1
GMLP_WIDTH = D_MODEL // 2
N_GMLP_GROUPS = 4
GMLP_GROUP = GMLP_WIDTH // N_GMLP_GROUPS
CHUNK = 128
D_FF = -(-8 * D_MODEL // (3 * 256)) * 256
IN_WIDTH = POOL_WIDTH + 2 * GMLP_WIDTH + 2 * D_MODEL
EPS = 1e-6

kernel_name = "pool_gmlp_gated_hybrid_step"


def rmsnorm(x, g):
    xf = x.astype(jnp.float32)
    y = xf * lax.rsqrt(jnp.mean(xf * xf, axis=-1, keepdims=True) + EPS)
    return (y * g.astype(jnp.float32)).astype(x.dtype)


def pool_mixer(a, pos0, w_pool, s_pool):
    B, T, _ = a.shape
    af = a.astype(jnp.float32)
    pos = pos0 + jnp.arange(T)
    outs = []
    for gi, w in enumerate(POOL_WINDOWS):
        xg = af[..., gi * POOL_GROUP:(gi + 1) * POOL_GROUP]
        cs = jnp.cumsum(jnp.pad(xg, ((0, 0), (w, 0), (0, 0))), axis=1)
        wsum = cs[:, w:] - cs[:, :T]
        cnt = jnp.minimum(w, pos + 1).astype(jnp.float32)[None, :, None]
        outs.append(wsum / cnt - xg)
    p = jnp.stack(outs, axis=2).astype(a.dtype)
    p = jnp.einsum('btgc,gcd->btgd', p, w_pool)
    return p.reshape(B, T, POOL_WIDTH) * s_pool


def spatial_gating(z, w_s, b_s, g_v):
    B, T, _ = z.shape
    u = z[..., :GMLP_WIDTH]
    v = rmsnorm(z[..., GMLP_WIDTH:], g_v)
    pad = (-T) % CHUNK
    nc = (T + pad) // CHUNK
    vc = jnp.pad(v, ((0, 0), (0, pad), (0, 0))).reshape(B, nc, CHUNK, N_GMLP_GROUPS, GMLP_GROUP)
    mask = jnp.tril(jnp.ones((CHUNK, CHUNK), dtype=bool))
    w = jnp.where(mask[None], w_s, jnp.zeros_like(w_s))
    s = jnp.einsum('gij,bnjgc->bnigc', w, vc) + jnp.transpose(b_s)[None, None, :, :, None]
    s = s.reshape(B, nc * CHUNK, GMLP_WIDTH)[:, :T]
    return u * s, v


def layer(x, pool_prev, w_in, g_mix, w_pool, s_pool, w_s, b_s, g_v,
          w_pool_out, w_gmlp_out, w_out, g_ffn, w_gate, w_up, w_down):
    T = x.shape[1]
    h = rmsnorm(x, g_mix)
    proj = h @ w_in
    o1 = POOL_WIDTH
    o2 = o1 + 2 * GMLP_WIDTH
    o3 = o2 + D_MODEL
    a = proj[..., :o1]
    z = jax.nn.gelu(proj[..., o1:o2])
    gate_a = jax.nn.sigmoid(proj[..., o2:o3])
    gate_b = jax.nn.sigmoid(proj[..., o3:])
    if pool_prev is None:
        seq_a, pos0 = a, 0
    else:
        seq_a, pos0 = jnp.concatenate([pool_prev.astype(a.dtype), a], axis=1), PAST_LEN - POOL_STATE
    pa = pool_mixer(seq_a, pos0, w_pool, s_pool)[:, -T:]
    new_pool = seq_a[:, -POOL_STATE:]
    sg, v = spatial_gating(z, w_s, b_s, g_v)
    merged = gate_a * (pa @ w_pool_out) + gate_b * (sg @ w_gmlp_out)
    x = x + merged @ w_out
    h2 = rmsnorm(x, g_ffn)
    x = x + (jax.nn.silu(h2 @ w_gate) * (h2 @ w_up)) @ w_down
    return x, new_pool, v


def setup_inputs(seed: int = 0) -> dict:
    key = jax.random.key(seed)
    ks = jax.random.split(key, 20)
    f32 = jnp.float32

    def nrm(k, shape, scale):
        return jax.random.normal(k, shape, f32) * scale

    def gain(k, shape):
        return 1.0 + 0.02 * jax.random.normal(k, shape, f32)

    return {
        "x_prompt": nrm(ks[0], (BATCH, SEQ, D_MODEL), 1.0),
        "x_sample": nrm(ks[1], (DEC_BATCH, DEC_SEQ, D_MODEL), 1.0),
        "state_pool": nrm(ks[2], (DEPTH, DEC_BATCH, POOL_STATE, POOL_WIDTH), 1.0),
        "w_in": nrm(ks[3], (DEPTH, D_MODEL, IN_WIDTH), D_MODEL ** -0.5),
        "g_mix": gain(ks[4], (DEPTH, D_MODEL)),
        "w_pool": nrm(ks[5], (DEPTH, N_POOL_GROUPS, POOL_GROUP, POOL_GROUP), POOL_GROUP ** -0.5),
        "s_pool": gain(ks[6], (DEPTH, POOL_WIDTH)),
        "w_s": nrm(ks[7], (DEPTH, N_GMLP_GROUPS, CHUNK, CHUNK), CHUNK ** -0.5),
        "b_s": gain(ks[8], (DEPTH, N_GMLP_GROUPS, CHUNK)),
        "g_v": gain(ks[9], (DEPTH, GMLP_WIDTH)),
        "w_pool_out": nrm(ks[10], (DEPTH, POOL_WIDTH, D_MODEL), POOL_WIDTH ** -0.5),
        "w_gmlp_out": nrm(ks[11], (DEPTH, GMLP_WIDTH, D_MODEL), GMLP_WIDTH ** -0.5),
        "w_out": nrm(ks[12], (DEPTH, D_MODEL, D_MODEL), D_MODEL ** -0.5),
        "g_ffn": gain(ks[13], (DEPTH, D_MODEL)),
        "w_gate": nrm(ks[14], (DEPTH, D_MODEL, D_FF), D_MODEL ** -0.5),
        "w_up": nrm(ks[15], (DEPTH, D_MODEL, D_FF), D_MODEL ** -0.5),
        "w_down": nrm(ks[16], (DEPTH, D_FF, D_MODEL), D_FF ** -0.5),
        "g_final": gain(ks[17], (D_MODEL,)),
    }


def reference(x_prompt, x_sample, state_pool, w_in, g_mix, w_pool, s_pool, w_s, b_s, g_v,
              w_pool_out, w_gmlp_out, w_out, g_ffn, w_gate, w_up, w_down, g_final):
    xp, xs = x_prompt, x_sample
    pools_p, pools_s, vs_s = [], [], []
    for l in range(DEPTH):
        params = (w_in[l], g_mix[l], w_pool[l], s_pool[l], w_s[l], b_s[l], g_v[l],
                  w_pool_out[l], w_gmlp_out[l], w_out[l], g_ffn[l], w_gate[l], w_up[l], w_down[l])
        xp, pool_p, _ = layer(xp, None, *params)
        xs, pool_s, v_s = layer(xs, state_pool[l], *params)
        pools_p.append(pool_p)
        pools_s.append(pool_s)
        vs_s.append(v_s)
    y_prompt = rmsnorm(xp, g_final)
    y_sample = rmsnorm(xs, g_final)
    new_pool_prompt = jnp.stack(pools_p, axis=0)
    new_pool_sample = jnp.stack(pools_s, axis=0)
    new_v_sample = jnp.stack(vs_s, axis=0)
    return (y_prompt, y_sample, new_pool_prompt, new_pool_sample, new_v_sample)
```

```python
import functools

import jax
import jax.numpy as jnp
from jax import lax
from jax.experimental import pallas as pl
from jax.experimental.pallas import tpu as pltpu

D_MODEL = 1024
POOL_WINDOWS = (2, 4, 8, 16)
POOL_WIDTH = 512
POOL_GROUP = 128
POOL_STATE = 15
GMLP_WIDTH = 512
N_GMLP_GROUPS = 4
GMLP_GROUP = 128
CHUNK = 128
D_FF = 2816
EPS = 1e-6

O1 = POOL_WIDTH
O2 = O1 + 2 * GMLP_WIDTH
O3 = O2 + D_MODEL
IN_WIDTH = O3 + D_MODEL

HALO = 16
ROW_TILE = 256
VMEM_LIMIT_BYTES = 56 * 1024 * 1024

F32 = jnp.float32
BF16 = jnp.bfloat16


def _dot(a, b):
    return jnp.dot(a, b, preferred_element_type=F32)


def _rms(x, g):
    ms = jnp.mean(x * x, axis=-1, keepdims=True)
    return x * lax.rsqrt(ms + EPS) * g


def _in_proj(x, g_mix_ref, w_in_ref):
    h = _rms(x, g_mix_ref[...]).astype(BF16)
    a = _dot(h, w_in_ref[:, 0:O1])
    z = jax.nn.gelu(_dot(h, w_in_ref[:, O1:O2]))
    gate_a = jax.nn.sigmoid(_dot(h, w_in_ref[:, O2:O3]))
    gate_b = jax.nn.sigmoid(_dot(h, w_in_ref[:, O3:IN_WIDTH]))
    return a, z, gate_a, gate_b


def _pool_project(p, w_pool_ref, s_pool_ref):
    pb = p.astype(BF16)
    lo = _dot(pb[:, 0:256], w_pool_ref[0])
    hi = _dot(pb[:, 256:512], w_pool_ref[1])
    return jnp.concatenate([lo, hi], axis=1) * s_pool_ref[...]


def _merge_and_ffn(x, pa, sg, gate_a, gate_b, w_po_ref, w_go_ref, w_out_ref,
                   g_ffn_ref, w_gate_ref, w_up_ref, w_down_ref, g_final_ref):
    merged = (gate_a * _dot(pa.astype(BF16), w_po_ref[...])
              + gate_b * _dot(sg.astype(BF16), w_go_ref[...]))
    x1 = x + _dot(merged.astype(BF16), w_out_ref[...])
    h2 = _rms(x1, g_ffn_ref[...]).astype(BF16)
    act = jax.nn.silu(_dot(h2, w_gate_ref[...])) * _dot(h2, w_up_ref[...])
    x2 = x1 + _dot(act.astype(BF16), w_down_ref[...])
    return _rms(x2, g_final_ref[...])


def _prompt_kernel(x_ref, w_in_ref, g_mix_ref, w_pool_ref, s_pool_ref, w_s_ref,
                   bias_ref, g_v_ref, w_po_ref, w_go_ref, w_out_ref, g_ffn_ref,
                   w_gate_ref, w_up_ref, w_down_ref, g_final_ref,
                   y_ref, pool_ref, abuf):
    tm = x_ref.shape[1]
    t = pl.program_id(1)
    x = x_ref[0]
    a, z, gate_a, gate_b = _in_proj(x, g_mix_ref, w_in_ref)

    @pl.when(t == 0)
    def _():
        abuf[0:HALO, :] = jnp.zeros((HALO, POOL_WIDTH), F32)

    abuf[HALO:HALO + tm, :] = a
    pos = lax.broadcasted_iota(jnp.int32, (tm, 1), 0) + t * tm
    parts = []
    for gi, w in enumerate(POOL_WINDOWS):
        lanes = slice(gi * POOL_GROUP, (gi + 1) * POOL_GROUP)
        xg = abuf[HALO:HALO + tm, lanes]
        wsum = xg
        for k in range(1, w):
            wsum = wsum + abuf[HALO - k:HALO - k + tm, lanes]
        inv_cnt = 1.0 / jnp.minimum(w, pos + 1).astype(F32)
        parts.append(wsum * inv_cnt - xg)
    tail = abuf[tm:tm + HALO, :]
    pool_ref[0] = tail
    abuf[0:HALO, :] = tail
    pa = _pool_project(jnp.concatenate(parts, axis=1), w_pool_ref, s_pool_ref)

    u = z[:, 0:GMLP_WIDTH]
    v = _rms(z[:, GMLP_WIDTH:], g_v_ref[...])
    vb = v.astype(BF16)
    causal = (lax.broadcasted_iota(jnp.int32, (CHUNK, CHUNK), 0)
              >= lax.broadcasted_iota(jnp.int32, (CHUNK, CHUNK), 1))
    w_causal = [jnp.where(causal, w_s_ref[g], jnp.zeros((CHUNK, CHUNK), BF16))
                for g in range(N_GMLP_GROUPS)]
    rows = []
    for c in range(tm // CHUNK):
        cols = [_dot(w_causal[g],
                     vb[c * CHUNK:(c + 1) * CHUNK, g * GMLP_GROUP:(g + 1) * GMLP_GROUP])
                for g in range(N_GMLP_GROUPS)]
        rows.append(jnp.concatenate(cols, axis=1) + bias_ref[...])
    sg = u * jnp.concatenate(rows, axis=0)

    y_ref[0] = _merge_and_ffn(x, pa, sg, gate_a, gate_b, w_po_ref, w_go_ref,
                              w_out_ref, g_ffn_ref, w_gate_ref, w_up_ref,
                              w_down_ref, g_final_ref)


def _sample_kernel(x_ref, state_ref, w_in_ref, g_mix_ref, w_pool_ref, s_pool_ref,
                   w_diag_ref, bias0_ref, g_v_ref, w_po_ref, w_go_ref, w_out_ref,
                   g_ffn_ref, w_gate_ref, w_up_ref, w_down_ref, g_final_ref,
                   y_ref, pool_ref, v_ref):
    x = x_ref[...]
    a, z, gate_a, gate_b = _in_proj(x, g_mix_ref, w_in_ref)

    parts = []
    for gi, w in enumerate(POOL_WINDOWS):
        xg = a[:, gi * POOL_GROUP:(gi + 1) * POOL_GROUP]
        wsum = xg
        for k in range(1, w):
            j = POOL_STATE - k
            lo = j * POOL_WIDTH + gi * POOL_GROUP
            wsum = wsum + state_ref[:, lo:lo + POOL_GROUP]
        parts.append(wsum * (1.0 / w) - xg)
    pa = _pool_project(jnp.concatenate(parts, axis=1), w_pool_ref, s_pool_ref)
    keep = (POOL_STATE - 1) * POOL_WIDTH
    pool_ref[:, 0:keep] = state_ref[:, POOL_WIDTH:]
    pool_ref[:, keep:] = a

    u = z[:, 0:GMLP_WIDTH]
    v = _rms(z[:, GMLP_WIDTH:], g_v_ref[...])
    v_ref[...] = v
    sg = u * (w_diag_ref[...] * v + bias0_ref[...])

    y_ref[...] = _merge_and_ffn(x, pa, sg, gate_a, gate_b, w_po_ref, w_go_ref,
                                w_out_ref, g_ffn_ref, w_gate_ref, w_up_ref,
                                w_down_ref, g_final_ref)


def _resident(shape):
    zeros = (0,) * len(shape)
    return pl.BlockSpec(shape, lambda b, t: zeros, pipeline_mode=pl.Buffered(1))


def kernel(x_prompt, x_sample, state_pool, w_in, g_mix, w_pool, s_pool, w_s, b_s, g_v,
           w_pool_out, w_gmlp_out, w_out, g_ffn, w_gate, w_up, w_down, g_final):
    batch, seq, _ = x_prompt.shape
    dec_batch = x_sample.shape[0]
    assert w_in.shape[0] == 1 and x_sample.shape[1] == 1
    assert seq % ROW_TILE == 0 and ROW_TILE % CHUNK == 0

    w_in_b = w_in[0].astype(BF16)
    zero_blk = jnp.zeros((POOL_GROUP, POOL_GROUP), F32)
    wp = w_pool[0]
    w_pool_b = jnp.stack([
        jnp.block([[wp[0], zero_blk], [zero_blk, wp[1]]]),
        jnp.block([[wp[2], zero_blk], [zero_blk, wp[3]]]),
    ]).astype(BF16)
    w_s_b = w_s[0].astype(BF16)
    w_po_b = w_pool_out[0].astype(BF16)
    w_go_b = w_gmlp_out[0].astype(BF16)
    w_out_b = w_out[0].astype(BF16)
    w_gate_b = w_gate[0].astype(BF16)
    w_up_b = w_up[0].astype(BF16)
    w_down_b = w_down[0].astype(BF16)
    g_mix2 = g_mix[0][None, :]
    s_pool2 = s_pool[0][None, :]
    g_v2 = g_v[0][None, :]
    g_ffn2 = g_ffn[0][None, :]
    g_final2 = g_final[None, :]
    bias_full = jnp.repeat(jnp.transpose(b_s[0]), GMLP_GROUP, axis=1)
    w_diag = jnp.repeat(w_s[0][:, 0, 0], GMLP_GROUP)[None, :]
    bias0 = bias_full[0:1, :]

    shared = (w_po_b, w_go_b, w_out_b, g_ffn2, w_gate_b, w_up_b, w_down_b, g_final2)

    n_tiles = seq // ROW_TILE
    prompt_inputs = (x_prompt, w_in_b, g_mix2, w_pool_b, s_pool2, w_s_b, bias_full,
                     g_v2) + shared
    in_specs = [pl.BlockSpec((1, ROW_TILE, D_MODEL), lambda b, t: (b, t, 0))]
    in_specs += [_resident(arr.shape) for arr in prompt_inputs[1:]]
    y_prompt, pool_p = pl.pallas_call(
        _prompt_kernel,
        grid=(batch, n_tiles),
        in_specs=in_specs,
        out_specs=[
            pl.BlockSpec((1, ROW_TILE, D_MODEL), lambda b, t: (b, t, 0)),
            pl.BlockSpec((1, HALO, POOL_WIDTH), lambda b, t: (b, 0, 0)),
        ],
        out_shape=[
            jax.ShapeDtypeStruct((batch, seq, D_MODEL), F32),
            jax.ShapeDtypeStruct((batch, HALO, POOL_WIDTH), F32),
        ],
        scratch_shapes=[pltpu.VMEM((HALO + ROW_TILE, POOL_WIDTH), F32)],
        compiler_params=pltpu.CompilerParams(
            dimension_semantics=("arbitrary", "arbitrary"),
            vmem_limit_bytes=VMEM_LIMIT_BYTES),
        name="prompt_layer",
    )(*prompt_inputs)

    state2d = state_pool[0].reshape(dec_batch, POOL_STATE * POOL_WIDTH)
    sample_inputs = (x_sample[:, 0, :], state2d, w_in_b, g_mix2, w_pool_b, s_pool2,
                     w_diag, bias0, g_v2) + shared
    y_s, pool_s, v_s = pl.pallas_call(
        _sample_kernel,
        out_shape=[
            jax.ShapeDtypeStruct((dec_batch, D_MODEL), F32),
            jax.ShapeDtypeStruct((dec_batch, POOL_STATE * POOL_WIDTH), F32),
            jax.ShapeDtypeStruct((dec_batch, GMLP_WIDTH), F32),
        ],
        compiler_params=pltpu.CompilerParams(vmem_limit_bytes=VMEM_LIMIT_BYTES),
        name="sample_layer",
    )(*sample_inputs)

    return (
        y_prompt,
        y_s[:, None, :],
        pool_p[None, :, HALO - POOL_STATE:, :],
        pool_s.reshape(1, dec_batch, POOL_STATE, POOL_WIDTH),
        v_s[None, :, None, :],
    )
```

```python
import jax
import jax.numpy as jnp
from jax import lax
from jax.experimental import pallas as pl
from jax.experimental.pallas import tpu as pltpu

D_MODEL = 1024
POOL_WINDOWS = (2, 4, 8, 16)
POOL_WIDTH = 512
POOL_GROUP = 128
POOL_STATE = 15
GMLP_WIDTH = 512
N_GMLP_GROUPS = 4
GMLP_GROUP = 128
CHUNK = 128
D_FF = 2816
EPS = 1e-6

O1 = POOL_WIDTH
O2 = O1 + 2 * GMLP_WIDTH
O3 = O2 + D_MODEL
IN_WIDTH = O3 + D_MODEL

HALO = 16
ROW_TILE = 256
VMEM_LIMIT_BYTES = 56 * 1024 * 1024

F32 = jnp.float32
BF16 = jnp.bfloat16


def _dot(a, b):
    return jnp.dot(a, b, preferred_element_type=F32)


def _rms(x, g):
    ms = jnp.mean(x * x, axis=-1, keepdims=True)
    return x * lax.rsqrt(ms + EPS) * g


def _sigmoid(x):
    return 0.5 * jnp.tanh(0.5 * x) + 0.5


def _silu(x):
    h = 0.5 * x
    return h * jnp.tanh(h) + h


def _in_proj(x, g_mix_ref, w_in_ref):
    h = _rms(x, g_mix_ref[...]).astype(BF16)
    a = _dot(h, w_in_ref[:, 0:O1])
    z = jax.nn.gelu(_dot(h, w_in_ref[:, O1:O2]))
    gate_a = _sigmoid(_dot(h, w_in_ref[:, O2:O3]))
    gate_b = _sigmoid(_dot(h, w_in_ref[:, O3:IN_WIDTH]))
    return a, z, gate_a, gate_b


def _window_sums(ext, w):
    s = ext
    k = 1
    while k < w:
        s = s + pltpu.roll(s, k, axis=0)
        k *= 2
    return s


def _pool_project(p, w_pool_ref, s_pool_ref):
    pb = p.astype(BF16)
    lo = _dot(pb[:, 0:256], w_pool_ref[0])
    hi = _dot(pb[:, 256:512], w_pool_ref[1])
    return jnp.concatenate([lo, hi], axis=1) * s_pool_ref[...]


def _mix_merge(x, pa, sg, gate_a, gate_b, w_po_ref, w_go_ref, w_out_ref):
    merged = (gate_a * _dot(pa.astype(BF16), w_po_ref[...])
              + gate_b * _dot(sg.astype(BF16), w_go_ref[...]))
    return x + _dot(merged.astype(BF16), w_out_ref[...])


def _ffn_and_final_norm(x1, g_ffn_ref, w_gate_ref, w_up_ref, w_down_ref, g_final_ref):
    h2 = _rms(x1, g_ffn_ref[...]).astype(BF16)
    act = _silu(_dot(h2, w_gate_ref[...])) * _dot(h2, w_up_ref[...])
    x2 = x1 + _dot(act.astype(BF16), w_down_ref[...])
    return _rms(x2, g_final_ref[...])


def _prompt_kernel(x_ref, w_in_ref, g_mix_ref, w_pool_ref, s_pool_ref, w_s_ref,
                   bias_ref, g_v_ref, w_po_ref, w_go_ref, w_out_ref, g_ffn_ref,
                   w_gate_ref, w_up_ref, w_down_ref, g_final_ref,
                   y_ref, pool_ref, halo_ref):
    tm = x_ref.shape[1]
    t = pl.program_id(1)
    x = x_ref[0]
    a, z, gate_a, gate_b = _in_proj(x, g_mix_ref, w_in_ref)

    @pl.when(t == 0)
    def _():
        halo_ref[...] = jnp.zeros((HALO, POOL_WIDTH), F32)

    ext = jnp.concatenate([halo_ref[...], a], axis=0)
    pos = lax.broadcasted_iota(jnp.int32, (tm, 1), 0) + t * tm
    parts = []
    for gi, w in enumerate(POOL_WINDOWS):
        lanes = slice(gi * POOL_GROUP, (gi + 1) * POOL_GROUP)
        wsum = _window_sums(ext[:, lanes], w)[HALO:]
        inv_cnt = 1.0 / jnp.minimum(w, pos + 1).astype(F32)
        parts.append(wsum * inv_cnt - a[:, lanes])
    tail = a[tm - HALO:, :]
    pool_ref[0] = tail
    halo_ref[...] = tail
    pa = _pool_project(jnp.concatenate(parts, axis=1), w_pool_ref, s_pool_ref)

    u = z[:, 0:GMLP_WIDTH]
    v = _rms(z[:, GMLP_WIDTH:], g_v_ref[...])
    vb = v.astype(BF16)
    causal = (lax.broadcasted_iota(jnp.int32, (CHUNK, CHUNK), 0)
              >= lax.broadcasted_iota(jnp.int32, (CHUNK, CHUNK), 1))
    w_causal = [jnp.where(causal, w_s_ref[g], jnp.zeros((CHUNK, CHUNK), BF16))
                for g in range(N_GMLP_GROUPS)]
    rows = []
    for c in range(tm // CHUNK):
        cols = [_dot(w_causal[g],
                     vb[c * CHUNK:(c + 1) * CHUNK, g * GMLP_GROUP:(g + 1) * GMLP_GROUP])
                for g in range(N_GMLP_GROUPS)]
        rows.append(jnp.concatenate(cols, axis=1) + bias_ref[...])
    sg = u * jnp.concatenate(rows, axis=0)

    x1 = _mix_merge(x, pa, sg, gate_a, gate_b, w_po_ref, w_go_ref, w_out_ref)
    y_ref[0] = _ffn_and_final_norm(x1, g_ffn_ref, w_gate_ref, w_up_ref, w_down_ref,
                                   g_final_ref)


def _sample_kernel(x_ref, state_ref, w_in_ref, g_mix_ref, w_pool_ref, s_pool_ref,
                   w_diag_ref, bias0_ref, g_v_ref, w_po_ref, w_go_ref, w_out_ref,
                   g_ffn_ref, w_gate_ref, w_up_ref, w_down_ref, g_final_ref,
                   y_ref, pool_ref, v_ref):
    x = x_ref[...]
    a, z, gate_a, gate_b = _in_proj(x, g_mix_ref, w_in_ref)

    parts = []
    for gi, w in enumerate(POOL_WINDOWS):
        lanes = slice(gi * POOL_GROUP, (gi + 1) * POOL_GROUP)
        xg = a[:, lanes]
        wsum = xg
        for k in range(1, w):
            wsum = wsum + state_ref[POOL_STATE - k, :, lanes]
        parts.append(wsum * (1.0 / w) - xg)
    pa = _pool_project(jnp.concatenate(parts, axis=1), w_pool_ref, s_pool_ref)
    pool_ref[0:POOL_STATE - 1] = state_ref[1:POOL_STATE]
    pool_ref[POOL_STATE - 1] = a

    u = z[:, 0:GMLP_WIDTH]
    v = _rms(z[:, GMLP_WIDTH:], g_v_ref[...])
    v_ref[...] = v
    sg = u * (w_diag_ref[...] * v + bias0_ref[...])

    x1 = _mix_merge(x, pa, sg, gate_a, gate_b, w_po_ref, w_go_ref, w_out_ref)
    y_ref[...] = _ffn_and_final_norm(x1, g_ffn_ref, w_gate_ref, w_up_ref, w_down_ref,
                                     g_final_ref)


def _resident(shape):
    zeros = (0,) * len(shape)
    return pl.BlockSpec(shape, lambda b, t: zeros, pipeline_mode=pl.Buffered(1))


def kernel(x_prompt, x_sample, state_pool, w_in, g_mix, w_pool, s_pool, w_s, b_s, g_v,
           w_pool_out, w_gmlp_out, w_out, g_ffn, w_gate, w_up, w_down, g_final):
    batch, seq, _ = x_prompt.shape
    dec_batch = x_sample.shape[0]
    assert w_in.shape[0] == 1 and x_sample.shape[1] == 1
    assert seq % ROW_TILE == 0 and ROW_TILE % CHUNK == 0

    w_in_b = w_in[0].astype(BF16)
    zero_blk = jnp.zeros((POOL_GROUP, POOL_GROUP), F32)
    wp = w_pool[0]
    w_pool_b = jnp.stack([
        jnp.block([[wp[0], zero_blk], [zero_blk, wp[1]]]),
        jnp.block([[wp[2], zero_blk], [zero_blk, wp[3]]]),
    ]).astype(BF16)
    w_s_b = w_s[0].astype(BF16)
    w_po_b = w_pool_out[0].astype(BF16)
    w_go_b = w_gmlp_out[0].astype(BF16)
    w_out_b = w_out[0].astype(BF16)
    w_gate_b = w_gate[0].astype(BF16)
    w_up_b = w_up[0].astype(BF16)
    w_down_b = w_down[0].astype(BF16)
    g_mix2 = g_mix[0][None, :]
    s_pool2 = s_pool[0][None, :]
    g_v2 = g_v[0][None, :]
    g_ffn2 = g_ffn[0][None, :]
    g_final2 = g_final[None, :]
    bias_full = jnp.repeat(jnp.transpose(b_s[0]), GMLP_GROUP, axis=1)
    w_diag = jnp.repeat(w_s[0][:, 0, 0], GMLP_GROUP)[None, :]
    bias0 = bias_full[0:1, :]

    shared = (w_po_b, w_go_b, w_out_b, g_ffn2, w_gate_b, w_up_b, w_down_b, g_final2)

    n_tiles = seq // ROW_TILE
    prompt_inputs = (x_prompt, w_in_b, g_mix2, w_pool_b, s_pool2, w_s_b, bias_full,
                     g_v2) + shared
    in_specs = [pl.BlockSpec((1, ROW_TILE, D_MODEL), lambda b, t: (b, t, 0))]
    in_specs += [_resident(arr.shape) for arr in prompt_inputs[1:]]
    y_prompt, pool_p = pl.pallas_call(
        _prompt_kernel,
        grid=(batch, n_tiles),
        in_specs=in_specs,
        out_specs=[
            pl.BlockSpec((1, ROW_TILE, D_MODEL), lambda b, t: (b, t, 0)),
            pl.BlockSpec((1, HALO, POOL_WIDTH), lambda b, t: (b, 0, 0)),
        ],
        out_shape=[
            jax.ShapeDtypeStruct((batch, seq, D_MODEL), F32),
            jax.ShapeDtypeStruct((batch, HALO, POOL_WIDTH), F32),
        ],
        scratch_shapes=[pltpu.VMEM((HALO, POOL_WIDTH), F32)],
        compiler_params=pltpu.CompilerParams(
            dimension_semantics=("arbitrary", "arbitrary"),
            vmem_limit_bytes=VMEM_LIMIT_BYTES),
        name="prompt_layer",
    )(*prompt_inputs)

    state_t = jnp.transpose(state_pool[0], (1, 0, 2))
    sample_inputs = (x_sample[:, 0, :], state_t, w_in_b, g_mix2, w_pool_b, s_pool2,
                     w_diag, bias0, g_v2) + shared
    y_s, pool_s, v_s = pl.pallas_call(
        _sample_kernel,
        out_shape=[
            jax.ShapeDtypeStruct((dec_batch, D_MODEL), F32),
            jax.ShapeDtypeStruct((POOL_STATE, dec_batch, POOL_WIDTH), F32),
            jax.ShapeDtypeStruct((dec_batch, GMLP_WIDTH), F32),
        ],
        compiler_params=pltpu.CompilerParams(vmem_limit_bytes=VMEM_LIMIT_BYTES),
        name="sample_layer",
    )(*sample_inputs)

    return (
        y_prompt,
        y_s[:, None, :],
        pool_p[None, :, HALO - POOL_STATE:, :],
        jnp.transpose(pool_s, (1, 0, 2))[None],
        v_s[None, :, None, :],
    )
```

```python
import functools

import jax
import jax.numpy as jnp
from jax import lax
from jax.experimental import pallas as pl
from jax.experimental.pallas import tpu as pltpu

D_MODEL = 1024
POOL_WINDOWS = (2, 4, 8, 16)
POOL_WIDTH = 512
POOL_GROUP = 128
POOL_STATE = 15
GMLP_WIDTH = 512
N_GMLP_GROUPS = 4
GMLP_GROUP = 128
CHUNK = 128
D_FF = 2816
EPS = 1e-6

O1 = POOL_WIDTH
O2 = O1 + 2 * GMLP_WIDTH
O3 = O2 + D_MODEL
IN_WIDTH = O3 + D_MODEL

HALO = 16
ROW_TILE = 256
FFN_CHUNK = 256
VMEM_LIMIT_BYTES = 56 * 1024 * 1024

F32 = jnp.float32
BF16 = jnp.bfloat16


def _dot(a, b):
    return jnp.dot(a, b, preferred_element_type=F32)


def _rms(x, g):
    ms = jnp.mean(x * x, axis=-1, keepdims=True)
    return x * lax.rsqrt(ms + EPS) * g


def _sigmoid(x):
    return 0.5 * jnp.tanh(0.5 * x) + 0.5


def _silu(x):
    h = 0.5 * x
    return h * jnp.tanh(h) + h


def _in_proj(x, g_mix_ref, w_in_ref):
    h = _rms(x, g_mix_ref[...]).astype(BF16)
    a = _dot(h, w_in_ref[:, 0:O1])
    z = jax.nn.gelu(_dot(h, w_in_ref[:, O1:O2]))
    gate_a = _sigmoid(_dot(h, w_in_ref[:, O2:O3]))
    gate_b = _sigmoid(_dot(h, w_in_ref[:, O3:IN_WIDTH]))
    return a, z, gate_a, gate_b


def _window_sums(ext, w):
    s = ext
    k = 1
    while k < w:
        s = s + pltpu.roll(s, k, axis=0)
        k *= 2
    return s


def _pool_project(p, w_pool_ref, s_pool_ref):
    pb = p.astype(BF16)
    lo = _dot(pb[:, 0:256], w_pool_ref[0])
    hi = _dot(pb[:, 256:512], w_pool_ref[1])
    return jnp.concatenate([lo, hi], axis=1) * s_pool_ref[...]


def _mix_merge(x, pa, sg, gate_a, gate_b, w_po_ref, w_go_ref, w_out_ref):
    merged = (gate_a * _dot(pa.astype(BF16), w_po_ref[...])
              + gate_b * _dot(sg.astype(BF16), w_go_ref[...]))
    return x + _dot(merged.astype(BF16), w_out_ref[...])


def _ffn_and_final_norm(x1, g_ffn_ref, w_gate_ref, w_up_ref, w_down_ref, g_final_ref):
    h2 = _rms(x1, g_ffn_ref[...]).astype(BF16)
    act = _silu(_dot(h2, w_gate_ref[...])) * _dot(h2, w_up_ref[...])
    x2 = x1 + _dot(act.astype(BF16), w_down_ref[...])
    return _rms(x2, g_final_ref[...])


def _prompt_kernel(x_ref, w_in_ref, g_mix_ref, w_pool_ref, s_pool_ref, w_s_ref,
                   bias_ref, g_v_ref, w_po_ref, w_go_ref, w_out_ref, g_ffn_ref,
                   w_gate_ref, w_up_ref, w_down_ref, g_final_ref,
                   y_ref, pool_ref, halo_ref, x1buf, h2buf, *, n_tiles, tiles_per_seq):
    tm = x_ref.shape[1]
    s = pl.program_id(0)
    half = D_MODEL // 2
    quarter = D_MODEL // 4

    @pl.when(s == 0)
    def _():
        x1buf[...] = jnp.zeros(x1buf.shape, F32)
        h2buf[...] = jnp.zeros(h2buf.shape, BF16)
        halo_ref[...] = jnp.zeros(halo_ref.shape, F32)

    h2 = h2buf[...]
    acts = {}
    ffn_acc = []

    def gate_up(c):
        cols = slice(c * FFN_CHUNK, (c + 1) * FFN_CHUNK)
        acts[c] = (_silu(_dot(h2, w_gate_ref[:, cols]))
                   * _dot(h2, w_up_ref[:, cols])).astype(BF16)

    def down(c):
        part = _dot(acts.pop(c), w_down_ref[c * FFN_CHUNK:(c + 1) * FFN_CHUNK, :])
        ffn_acc.append(part + (ffn_acc[-1] if ffn_acc else x1buf[...]))

    t = lax.rem(jnp.minimum(s, n_tiles - 1), tiles_per_seq)
    x = x_ref[0]

    gate_up(0)
    h = _rms(x, g_mix_ref[...]).astype(BF16)
    gate_up(1)
    a = _dot(h, w_in_ref[:, 0:O1])
    down(0)
    gate_up(2)
    z_v = _dot(h, w_in_ref[:, O1 + GMLP_WIDTH:O2])
    down(1)
    gate_up(3)

    halo = jnp.where(t == 0, 0.0, halo_ref[...])
    ext = jnp.concatenate([halo, a], axis=0)
    pos = lax.broadcasted_iota(jnp.int32, (tm, 1), 0) + t * tm
    parts = []
    for gi, w in enumerate(POOL_WINDOWS):
        lanes = slice(gi * POOL_GROUP, (gi + 1) * POOL_GROUP)
        wsum = _window_sums(ext[:, lanes], w)[HALO:]
        inv_cnt = 1.0 / jnp.minimum(w, pos + 1).astype(F32)
        parts.append(wsum * inv_cnt - a[:, lanes])
    tail = a[tm - HALO:, :]
    pool_ref[0] = tail
    halo_ref[...] = tail
    pb = jnp.concatenate(parts, axis=1).astype(BF16)

    z_u = _dot(h, w_in_ref[:, O1:O1 + GMLP_WIDTH])
    down(2)
    gate_up(4)
    vb = _rms(jax.nn.gelu(z_v), g_v_ref[...]).astype(BF16)
    pa = (jnp.concatenate([_dot(pb[:, 0:256], w_pool_ref[0]),
                           _dot(pb[:, 256:512], w_pool_ref[1])], axis=1)
          * s_pool_ref[...]).astype(BF16)
    gate_a = [_sigmoid(_dot(h, w_in_ref[:, O2 + i * quarter:O2 + (i + 1) * quarter]))
              for i in range(2)]
    down(3)
    gate_up(5)
    u = jax.nn.gelu(z_u)

    causal = (lax.broadcasted_iota(jnp.int32, (CHUNK, CHUNK), 0)
              >= lax.broadcasted_iota(jnp.int32, (CHUNK, CHUNK), 1))
    w_causal = [jnp.where(causal, w_s_ref[g], jnp.zeros((CHUNK, CHUNK), BF16))
                for g in range(N_GMLP_GROUPS)]
    pair_w = 2 * GMLP_GROUP
    w_pairs = [jnp.concatenate(w_causal[2 * p:2 * p + 2], axis=1)
               for p in range(N_GMLP_GROUPS // 2)]
    first = lax.broadcasted_iota(jnp.int32, (CHUNK, pair_w), 1) < GMLP_GROUP
    no_v = jnp.zeros((CHUNK, pair_w), BF16)
    rows = []
    for c in range(tm // CHUNK):
        cols = []
        for p in range(N_GMLP_GROUPS // 2):
            v_pair = vb[c * CHUNK:(c + 1) * CHUNK, p * pair_w:(p + 1) * pair_w]
            v_diag = jnp.concatenate([jnp.where(first, v_pair, no_v),
                                      jnp.where(first, no_v, v_pair)], axis=0)
            cols.append(_dot(w_pairs[p], v_diag))
        rows.append(jnp.concatenate(cols, axis=1) + bias_ref[...])
    gate_a += [_sigmoid(_dot(h, w_in_ref[:, O2 + i * quarter:O2 + (i + 1) * quarter]))
               for i in range(2, 4)]
    down(4)
    gate_up(6)
    sg = (u * jnp.concatenate(rows, axis=0)).astype(BF16)
    gate_b = [_sigmoid(_dot(h, w_in_ref[:, O3 + i * quarter:O3 + (i + 1) * quarter]))
              for i in range(2)]
    down(5)
    gate_up(7)
    gate_b += [_sigmoid(_dot(h, w_in_ref[:, O3 + i * quarter:O3 + (i + 1) * quarter]))
               for i in range(2, 4)]
    po_lo = _dot(pa, w_po_ref[:, 0:half])
    go_lo = _dot(sg, w_go_ref[:, 0:half])
    down(6)
    gate_up(8)
    merged_lo = (jnp.concatenate(gate_a[0:2], axis=1) * po_lo
                 + jnp.concatenate(gate_b[0:2], axis=1) * go_lo).astype(BF16)
    po_hi = _dot(pa, w_po_ref[:, half:])
    go_hi = _dot(sg, w_go_ref[:, half:])
    down(7)
    gate_up(9)
    merged_hi = (jnp.concatenate(gate_a[2:4], axis=1) * po_hi
                 + jnp.concatenate(gate_b[2:4], axis=1) * go_hi).astype(BF16)
    out_lo = _dot(merged_lo, w_out_ref[0:half, :])
    down(8)
    gate_up(10)
    x1 = x + out_lo + _dot(merged_hi, w_out_ref[half:, :])
    down(9)
    x1buf[...] = x1
    h2buf[...] = _rms(x1, g_ffn_ref[...]).astype(BF16)
    last = D_FF // FFN_CHUNK - 1
    act_last = acts.pop(last)
    for r in range(2):
        rows_r = slice(r * (tm // 2), (r + 1) * (tm // 2))
        x2 = ffn_acc[-1][rows_r] + _dot(act_last[rows_r], w_down_ref[last * FFN_CHUNK:, :])
        y_ref[0, rows_r, :] = _rms(x2, g_final_ref[...])


def _sample_kernel(x_ref, state_ref, w_in_ref, g_mix_ref, w_pool_ref, s_pool_ref,
                   w_diag_ref, bias0_ref, g_v_ref, w_po_ref, w_go_ref, w_out_ref,
                   g_ffn_ref, w_gate_ref, w_up_ref, w_down_ref, g_final_ref,
                   y_ref, pool_ref, v_ref):
    x = x_ref[...]
    a, z, gate_a, gate_b = _in_proj(x, g_mix_ref, w_in_ref)

    parts = []
    for gi, w in enumerate(POOL_WINDOWS):
        lanes = slice(gi * POOL_GROUP, (gi + 1) * POOL_GROUP)
        xg = a[:, lanes]
        wsum = xg
        for k in range(1, w):
            wsum = wsum + state_ref[POOL_STATE - k, :, lanes]
        parts.append(wsum * (1.0 / w) - xg)
    pa = _pool_project(jnp.concatenate(parts, axis=1), w_pool_ref, s_pool_ref)
    pool_ref[0:POOL_STATE - 1] = state_ref[1:POOL_STATE]
    pool_ref[POOL_STATE - 1] = a

    u = z[:, 0:GMLP_WIDTH]
    v = _rms(z[:, GMLP_WIDTH:], g_v_ref[...])
    v_ref[...] = v
    sg = u * (w_diag_ref[...] * v + bias0_ref[...])

    x1 = _mix_merge(x, pa, sg, gate_a, gate_b, w_po_ref, w_go_ref, w_out_ref)
    y_ref[...] = _ffn_and_final_norm(x1, g_ffn_ref, w_gate_ref, w_up_ref, w_down_ref,
                                     g_final_ref)


def _resident(shape):
    zeros = (0,) * len(shape)
    return pl.BlockSpec(shape, lambda s: zeros, pipeline_mode=pl.Buffered(1))


def kernel(x_prompt, x_sample, state_pool, w_in, g_mix, w_pool, s_pool, w_s, b_s, g_v,
           w_pool_out, w_gmlp_out, w_out, g_ffn, w_gate, w_up, w_down, g_final):
    batch, seq, _ = x_prompt.shape
    dec_batch = x_sample.shape[0]
    assert w_in.shape[0] == 1 and x_sample.shape[1] == 1
    assert seq % ROW_TILE == 0 and ROW_TILE % CHUNK == 0
    assert D_FF == 11 * FFN_CHUNK

    w_in_b = w_in[0].astype(BF16)
    zero_blk = jnp.zeros((POOL_GROUP, POOL_GROUP), F32)
    wp = w_pool[0]
    w_pool_b = jnp.stack([
        jnp.block([[wp[0], zero_blk], [zero_blk, wp[1]]]),
        jnp.block([[wp[2], zero_blk], [zero_blk, wp[3]]]),
    ]).astype(BF16)
    w_s_b = w_s[0].astype(BF16)
    w_po_b = w_pool_out[0].astype(BF16)
    w_go_b = w_gmlp_out[0].astype(BF16)
    w_out_b = w_out[0].astype(BF16)
    w_gate_b = w_gate[0].astype(BF16)
    w_up_b = w_up[0].astype(BF16)
    w_down_b = w_down[0].astype(BF16)
    g_mix2 = g_mix[0][None, :]
    s_pool2 = s_pool[0][None, :]
    g_v2 = g_v[0][None, :]
    g_ffn2 = g_ffn[0][None, :]
    g_final2 = g_final[None, :]
    bias_full = jnp.repeat(jnp.transpose(b_s[0]), GMLP_GROUP, axis=1)
    w_diag = jnp.repeat(w_s[0][:, 0, 0], GMLP_GROUP)[None, :]
    bias0 = bias_full[0:1, :]

    shared = (w_po_b, w_go_b, w_out_b, g_ffn2, w_gate_b, w_up_b, w_down_b, g_final2)

    tiles_per_seq = seq // ROW_TILE
    n_tiles = batch * tiles_per_seq

    def mixer_tile(s):
        i = jnp.minimum(s, n_tiles - 1)
        return i // tiles_per_seq, i % tiles_per_seq

    def ffn_tile(s):
        i = jnp.maximum(s - 1, 0)
        return i // tiles_per_seq, i % tiles_per_seq

    prompt_inputs = (x_prompt, w_in_b, g_mix2, w_pool_b, s_pool2, w_s_b, bias_full,
                     g_v2) + shared
    in_specs = [pl.BlockSpec((1, ROW_TILE, D_MODEL), lambda s: (*mixer_tile(s), 0))]
    in_specs += [_resident(arr.shape) for arr in prompt_inputs[1:]]
    y_prompt, pool_p = pl.pallas_call(
        functools.partial(_prompt_kernel, n_tiles=n_tiles, tiles_per_seq=tiles_per_seq),
        grid=(n_tiles + 1,),
        in_specs=in_specs,
        out_specs=[
            pl.BlockSpec((1, ROW_TILE, D_MODEL), lambda s: (*ffn_tile(s), 0)),
            pl.BlockSpec((1, HALO, POOL_WIDTH), lambda s: (mixer_tile(s)[0], 0, 0)),
        ],
        out_shape=[
            jax.ShapeDtypeStruct((batch, seq, D_MODEL), F32),
            jax.ShapeDtypeStruct((batch, HALO, POOL_WIDTH), F32),
        ],
        scratch_shapes=[
            pltpu.VMEM((HALO, POOL_WIDTH), F32),
            pltpu.VMEM((ROW_TILE, D_MODEL), F32),
            pltpu.VMEM((ROW_TILE, D_MODEL), BF16),
        ],
        compiler_params=pltpu.CompilerParams(
            dimension_semantics=("arbitrary",),
            vmem_limit_bytes=VMEM_LIMIT_BYTES),
        name="prompt_layer",
    )(*prompt_inputs)

    state_t = jnp.transpose(state_pool[0], (1, 0, 2))
    sample_inputs = (x_sample[:, 0, :], state_t, w_in_b, g_mix2, w_pool_b, s_pool2,
                     w_diag, bias0, g_v2) + shared
    y_s, pool_s, v_s = pl.pallas_call(
        _sample_kernel,
        out_shape=[
            jax.ShapeDtypeStruct((dec_batch, D_MODEL), F32),
            jax.ShapeDtypeStruct((POOL_STATE, dec_batch, POOL_WIDTH), F32),
            jax.ShapeDtypeStruct((dec_batch, GMLP_WIDTH), F32),
        ],
        compiler_params=pltpu.CompilerParams(vmem_limit_bytes=VMEM_LIMIT_BYTES),
        name="sample_layer",
    )(*sample_inputs)

    return (
        y_prompt,
        y_s[:, None, :],
        pool_p[None, :, HALO - POOL_STATE:, :],
        jnp.transpose(pool_s, (1, 0, 2))[None],
        v_s[None, :, None, :],
    )
```

```python
import functools

import jax
import jax.numpy as jnp
from jax import lax
from jax.experimental import pallas as pl
from jax.experimental.pallas import tpu as pltpu

D_MODEL = 1024
POOL_WINDOWS = (2, 4, 8, 16)
POOL_WIDTH = 512
POOL_GROUP = 128
POOL_STATE = 15
GMLP_WIDTH = 512
N_GMLP_GROUPS = 4
GMLP_GROUP = 128
CHUNK = 128
D_FF = 2816
EPS = 1e-6

O1 = POOL_WIDTH
O2 = O1 + 2 * GMLP_WIDTH
O3 = O2 + D_MODEL
IN_WIDTH = O3 + D_MODEL

HALO = 16
ROW_TILE = 256
FFN_CHUNK = 256
N_FFN_CHUNKS = D_FF // FFN_CHUNK
VMEM_LIMIT_BYTES = 56 * 1024 * 1024

F32 = jnp.float32
BF16 = jnp.bfloat16


def _dot(a, b):
    return jnp.dot(a, b, preferred_element_type=F32)


def _rms(x, g):
    ms = jnp.mean(x * x, axis=-1, keepdims=True)
    return x * lax.rsqrt(ms + EPS) * g


def _sigmoid(x):
    return 0.5 * jnp.tanh(0.5 * x) + 0.5


def _silu(x):
    h = 0.5 * x
    return h * jnp.tanh(h) + h


def _window_sums(ext, w):
    s = ext
    k = 1
    while k < w:
        s = s + pltpu.roll(s, k, axis=0)
        k *= 2
    return s


def _pool_tile(a, t, halo, halo_ref, pool_ref):
    tm = a.shape[0]
    ext = jnp.concatenate([halo, a], axis=0)
    pos = lax.broadcasted_iota(jnp.int32, (tm, 1), 0) + t * tm
    parts = []
    for gi, w in enumerate(POOL_WINDOWS):
        lanes = slice(gi * POOL_GROUP, (gi + 1) * POOL_GROUP)
        wsum = _window_sums(ext[:, lanes], w)[HALO:]
        inv_cnt = 1.0 / jnp.minimum(w, pos + 1).astype(F32)
        parts.append(wsum * inv_cnt - a[:, lanes])
    tail = a[tm - HALO:, :]
    pool_ref[0] = tail
    halo_ref[...] = tail
    return jnp.concatenate(parts, axis=1).astype(BF16)


def _pool_project(pb, w_pool_ref, s_pool_ref):
    lo = _dot(pb[:, 0:256], w_pool_ref[0])
    hi = _dot(pb[:, 256:512], w_pool_ref[1])
    return (jnp.concatenate([lo, hi], axis=1) * s_pool_ref[...]).astype(BF16)


def _spatial_gate(vb, w_s_ref, bias_ref):
    causal = (lax.broadcasted_iota(jnp.int32, (CHUNK, CHUNK), 0)
              >= lax.broadcasted_iota(jnp.int32, (CHUNK, CHUNK), 1))
    w_causal = [jnp.where(causal, w_s_ref[g], jnp.zeros((CHUNK, CHUNK), BF16))
                for g in range(N_GMLP_GROUPS)]
    pair_w = 2 * GMLP_GROUP
    w_pairs = [jnp.concatenate(w_causal[2 * p:2 * p + 2], axis=1)
               for p in range(N_GMLP_GROUPS // 2)]
    first = lax.broadcasted_iota(jnp.int32, (CHUNK, pair_w), 1) < GMLP_GROUP
    no_v = jnp.zeros((CHUNK, pair_w), BF16)
    rows = []
    for c in range(vb.shape[0] // CHUNK):
        cols = []
        for p in range(N_GMLP_GROUPS // 2):
            v_pair = vb[c * CHUNK:(c + 1) * CHUNK, p * pair_w:(p + 1) * pair_w]
            v_diag = jnp.concatenate([jnp.where(first, v_pair, no_v),
                                      jnp.where(first, no_v, v_pair)], axis=0)
            cols.append(_dot(w_pairs[p], v_diag))
        rows.append(jnp.concatenate(cols, axis=1) + bias_ref[...])
    return jnp.concatenate(rows, axis=0)


def _mixer_first_step(x_ref, w_in_ref, g_mix_ref, w_pool_ref, s_pool_ref, w_s_ref,
                      bias_ref, g_v_ref, w_po_ref, w_go_ref, w_out_ref, g_ffn_ref,
                      pool_ref, halo_ref, x1buf, h2buf):
    x = x_ref[0]
    h = _rms(x, g_mix_ref[...]).astype(BF16)
    a = _dot(h, w_in_ref[:, 0:O1])
    z_v = _dot(h, w_in_ref[:, O1 + GMLP_WIDTH:O2])
    z_u = _dot(h, w_in_ref[:, O1:O1 + GMLP_WIDTH])
    pb = _pool_tile(a, 0, jnp.zeros((HALO, POOL_WIDTH), F32), halo_ref, pool_ref)
    gate_a = _sigmoid(_dot(h, w_in_ref[:, O2:O3]))
    vb = _rms(jax.nn.gelu(z_v), g_v_ref[...]).astype(BF16)
    gate_b = _sigmoid(_dot(h, w_in_ref[:, O3:IN_WIDTH]))
    pa = _pool_project(pb, w_pool_ref, s_pool_ref)
    sg = (jax.nn.gelu(z_u) * _spatial_gate(vb, w_s_ref, bias_ref)).astype(BF16)
    merged = (gate_a * _dot(pa, w_po_ref[...])
              + gate_b * _dot(sg, w_go_ref[...])).astype(BF16)
    x1 = x + _dot(merged, w_out_ref[...])
    x1buf[...] = x1
    h2buf[...] = _rms(x1, g_ffn_ref[...]).astype(BF16)


def _ffn_last_step(w_gate_ref, w_up_ref, w_down_ref, g_final_ref, y_ref, x1buf, h2buf):
    h2 = h2buf[...]
    act = (_silu(_dot(h2, w_gate_ref[...])) * _dot(h2, w_up_ref[...])).astype(BF16)
    x2 = x1buf[...] + _dot(act, w_down_ref[...])
    y_ref[0] = _rms(x2, g_final_ref[...])


def _steady_step(t, x_ref, w_in_ref, g_mix_ref, w_pool_ref, s_pool_ref, w_s_ref,
                 bias_ref, g_v_ref, w_po_ref, w_go_ref, w_out_ref, g_ffn_ref,
                 w_gate_ref, w_up_ref, w_down_ref, g_final_ref,
                 y_ref, pool_ref, halo_ref, x1buf, h2buf):
    tm = x_ref.shape[1]
    half = D_MODEL // 2
    quarter = D_MODEL // 4

    h2 = h2buf[...]
    acts = {}
    ffn_acc = []

    def gate_up(c):
        cols = slice(c * FFN_CHUNK, (c + 1) * FFN_CHUNK)
        acts[c] = (_silu(_dot(h2, w_gate_ref[:, cols]))
                   * _dot(h2, w_up_ref[:, cols])).astype(BF16)

    def down(c):
        part = _dot(acts.pop(c), w_down_ref[c * FFN_CHUNK:(c + 1) * FFN_CHUNK, :])
        ffn_acc.append(part + (ffn_acc[-1] if ffn_acc else x1buf[...]))

    def in_gate(base, i):
        return _sigmoid(_dot(h, w_in_ref[:, base + i * quarter:base + (i + 1) * quarter]))

    x = x_ref[0]
    gate_up(0)
    h = _rms(x, g_mix_ref[...]).astype(BF16)
    gate_up(1)
    a = _dot(h, w_in_ref[:, 0:O1])
    down(0)
    gate_up(2)
    z_v = _dot(h, w_in_ref[:, O1 + GMLP_WIDTH:O2])
    down(1)
    gate_up(3)
    pb = _pool_tile(a, t, jnp.where(t == 0, 0.0, halo_ref[...]), halo_ref, pool_ref)
    z_u = _dot(h, w_in_ref[:, O1:O1 + GMLP_WIDTH])
    down(2)
    gate_up(4)
    vb = _rms(jax.nn.gelu(z_v), g_v_ref[...]).astype(BF16)
    pa = _pool_project(pb, w_pool_ref, s_pool_ref)
    gate_a = [in_gate(O2, 0), in_gate(O2, 1)]
    down(3)
    gate_up(5)
    u = jax.nn.gelu(z_u)
    s_gate = _spatial_gate(vb, w_s_ref, bias_ref)
    gate_a += [in_gate(O2, 2), in_gate(O2, 3)]
    down(4)
    gate_up(6)
    sg = (u * s_gate).astype(BF16)
    gate_b = [in_gate(O3, 0), in_gate(O3, 1)]
    down(5)
    gate_up(7)
    gate_b += [in_gate(O3, 2), in_gate(O3, 3)]
    po_lo = _dot(pa, w_po_ref[:, 0:half])
    go_lo = _dot(sg, w_go_ref[:, 0:half])
    down(6)
    gate_up(8)
    merged_lo = (jnp.concatenate(gate_a[0:2], axis=1) * po_lo
                 + jnp.concatenate(gate_b[0:2], axis=1) * go_lo).astype(BF16)
    po_hi = _dot(pa, w_po_ref[:, half:])
    go_hi = _dot(sg, w_go_ref[:, half:])
    down(7)
    gate_up(9)
    merged_hi = (jnp.concatenate(gate_a[2:4], axis=1) * po_hi
                 + jnp.concatenate(gate_b[2:4], axis=1) * go_hi).astype(BF16)
    out_lo = _dot(merged_lo, w_out_ref[0:half, :])
    down(8)
    gate_up(10)
    x1 = x + out_lo + _dot(merged_hi, w_out_ref[half:, :])
    down(9)
    x1buf[...] = x1
    h2buf[...] = _rms(x1, g_ffn_ref[...]).astype(BF16)
    down(10)
    y_ref[0] = _rms(ffn_acc[-1], g_final_ref[...])


def _prompt_kernel(x_ref, w_in_ref, g_mix_ref, w_pool_ref, s_pool_ref, w_s_ref,
                   bias_ref, g_v_ref, w_po_ref, w_go_ref, w_out_ref, g_ffn_ref,
                   w_gate_ref, w_up_ref, w_down_ref, g_final_ref,
                   y_ref, pool_ref, halo_ref, x1buf, h2buf, *, n_tiles, tiles_per_seq):
    s = pl.program_id(0)
    mixer_refs = (x_ref, w_in_ref, g_mix_ref, w_pool_ref, s_pool_ref, w_s_ref, bias_ref,
                  g_v_ref, w_po_ref, w_go_ref, w_out_ref, g_ffn_ref)
    ffn_refs = (w_gate_ref, w_up_ref, w_down_ref, g_final_ref)
    carry = (halo_ref, x1buf, h2buf)

    @pl.when(s == 0)
    def _():
        _mixer_first_step(*mixer_refs, pool_ref, *carry)

    @pl.when(jnp.logical_and(s > 0, s < n_tiles))
    def _():
        _steady_step(lax.rem(s, tiles_per_seq), *mixer_refs, *ffn_refs,
                     y_ref, pool_ref, *carry)

    @pl.when(s == n_tiles)
    def _():
        _ffn_last_step(*ffn_refs, y_ref, x1buf, h2buf)


def _sample_kernel(x_ref, state_ref, w_in_ref, g_mix_ref, w_pool_ref, s_pool_ref,
                   w_diag_ref, bias0_ref, g_v_ref, w_po_ref, w_go_ref, w_out_ref,
                   g_ffn_ref, w_gate_ref, w_up_ref, w_down_ref, g_final_ref,
                   y_ref, pool_ref, v_ref):
    x = x_ref[...]
    h = _rms(x, g_mix_ref[...]).astype(BF16)
    a = _dot(h, w_in_ref[:, 0:O1])
    z = jax.nn.gelu(_dot(h, w_in_ref[:, O1:O2]))
    gate_a = _sigmoid(_dot(h, w_in_ref[:, O2:O3]))
    gate_b = _sigmoid(_dot(h, w_in_ref[:, O3:IN_WIDTH]))

    parts = []
    for gi, w in enumerate(POOL_WINDOWS):
        lanes = slice(gi * POOL_GROUP, (gi + 1) * POOL_GROUP)
        xg = a[:, lanes]
        wsum = xg
        for k in range(1, w):
            wsum = wsum + state_ref[POOL_STATE - k, :, lanes]
        parts.append(wsum * (1.0 / w) - xg)
    pa = _pool_project(jnp.concatenate(parts, axis=1).astype(BF16), w_pool_ref, s_pool_ref)
    pool_ref[0:POOL_STATE - 1] = state_ref[1:POOL_STATE]
    pool_ref[POOL_STATE - 1] = a

    u = z[:, 0:GMLP_WIDTH]
    v = _rms(z[:, GMLP_WIDTH:], g_v_ref[...])
    v_ref[...] = v
    sg = (u * (w_diag_ref[...] * v + bias0_ref[...])).astype(BF16)

    merged = (gate_a * _dot(pa, w_po_ref[...])
              + gate_b * _dot(sg, w_go_ref[...])).astype(BF16)
    x1 = x + _dot(merged, w_out_ref[...])
    h2 = _rms(x1, g_ffn_ref[...]).astype(BF16)
    act = (_silu(_dot(h2, w_gate_ref[...])) * _dot(h2, w_up_ref[...])).astype(BF16)
    x2 = x1 + _dot(act, w_down_ref[...])
    y_ref[...] = _rms(x2, g_final_ref[...])


def _resident(shape):
    zeros = (0,) * len(shape)
    return pl.BlockSpec(shape, lambda s: zeros, pipeline_mode=pl.Buffered(1))


def kernel(x_prompt, x_sample, state_pool, w_in, g_mix, w_pool, s_pool, w_s, b_s, g_v,
           w_pool_out, w_gmlp_out, w_out, g_ffn, w_gate, w_up, w_down, g_final):
    batch, seq, _ = x_prompt.shape
    dec_batch = x_sample.shape[0]
    assert w_in.shape[0] == 1 and x_sample.shape[1] == 1
    assert seq % ROW_TILE == 0 and ROW_TILE % CHUNK == 0
    assert N_FFN_CHUNKS == 11

    w_in_b = w_in[0].astype(BF16)
    zero_blk = jnp.zeros((POOL_GROUP, POOL_GROUP), F32)
    wp = w_pool[0]
    w_pool_b = jnp.stack([
        jnp.block([[wp[0], zero_blk], [zero_blk, wp[1]]]),
        jnp.block([[wp[2], zero_blk], [zero_blk, wp[3]]]),
    ]).astype(BF16)
    w_s_b = w_s[0].astype(BF16)
    w_po_b = w_pool_out[0].astype(BF16)
    w_go_b = w_gmlp_out[0].astype(BF16)
    w_out_b = w_out[0].astype(BF16)
    w_gate_b = w_gate[0].astype(BF16)
    w_up_b = w_up[0].astype(BF16)
    w_down_b = w_down[0].astype(BF16)
    g_mix2 = g_mix[0][None, :]
    s_pool2 = s_pool[0][None, :]
    g_v2 = g_v[0][None, :]
    g_ffn2 = g_ffn[0][None, :]
    g_final2 = g_final[None, :]
    bias_full = jnp.repeat(jnp.transpose(b_s[0]), GMLP_GROUP, axis=1)
    w_diag = jnp.repeat(w_s[0][:, 0, 0], GMLP_GROUP)[None, :]
    bias0 = bias_full[0:1, :]

    shared = (w_po_b, w_go_b, w_out_b, g_ffn2, w_gate_b, w_up_b, w_down_b, g_final2)

    tiles_per_seq = seq // ROW_TILE
    n_tiles = batch * tiles_per_seq

    def mixer_tile(s):
        i = jnp.minimum(s, n_tiles - 1)
        return i // tiles_per_seq, i % tiles_per_seq

    def ffn_tile(s):
        i = jnp.maximum(s - 1, 0)
        return i // tiles_per_seq, i % tiles_per_seq

    prompt_inputs = (x_prompt, w_in_b, g_mix2, w_pool_b, s_pool2, w_s_b, bias_full,
                     g_v2) + shared
    in_specs = [pl.BlockSpec((1, ROW_TILE, D_MODEL), lambda s: (*mixer_tile(s), 0))]
    in_specs += [_resident(arr.shape) for arr in prompt_inputs[1:]]
    y_prompt, pool_p = pl.pallas_call(
        functools.partial(_prompt_kernel, n_tiles=n_tiles, tiles_per_seq=tiles_per_seq),
        grid=(n_tiles + 1,),
        in_specs=in_specs,
        out_specs=[
            pl.BlockSpec((1, ROW_TILE, D_MODEL), lambda s: (*ffn_tile(s), 0)),
            pl.BlockSpec((1, HALO, POOL_WIDTH), lambda s: (mixer_tile(s)[0], 0, 0)),
        ],
        out_shape=[
            jax.ShapeDtypeStruct((batch, seq, D_MODEL), F32),
            jax.ShapeDtypeStruct((batch, HALO, POOL_WIDTH), F32),
        ],
        scratch_shapes=[
            pltpu.VMEM((HALO, POOL_WIDTH), F32),
            pltpu.VMEM((ROW_TILE, D_MODEL), F32),
            pltpu.VMEM((ROW_TILE, D_MODEL), BF16),
        ],
        compiler_params=pltpu.CompilerParams(
            dimension_semantics=("arbitrary",),
            vmem_limit_bytes=VMEM_LIMIT_BYTES),
        name="prompt_layer",
    )(*prompt_inputs)

    state_t = jnp.transpose(state_pool[0], (1, 0, 2))
    sample_inputs = (x_sample[:, 0, :], state_t, w_in_b, g_mix2, w_pool_b, s_pool2,
                     w_diag, bias0, g_v2) + shared
    y_s, pool_s, v_s = pl.pallas_call(
        _sample_kernel,
        out_shape=[
            jax.ShapeDtypeStruct((dec_batch, D_MODEL), F32),
            jax.ShapeDtypeStruct((POOL_STATE, dec_batch, POOL_WIDTH), F32),
            jax.ShapeDtypeStruct((dec_batch, GMLP_WIDTH), F32),
        ],
        compiler_params=pltpu.CompilerParams(vmem_limit_bytes=VMEM_LIMIT_BYTES),
        name="sample_layer",
    )(*sample_inputs)

    return (
        y_prompt,
        y_s[:, None, :],
        pool_p[None, :, HALO - POOL_STATE:, :],
        jnp.transpose(pool_s, (1, 0, 2))[None],
        v_s[None, :, None, :],
    )
```

```python
import functools

import jax
import jax.numpy as jnp
from jax import lax
from jax.experimental import pallas as pl
from jax.experimental.pallas import tpu as pltpu

D_MODEL = 1024
POOL_WINDOWS = (2, 4, 8, 16)
POOL_WIDTH = 512
POOL_GROUP = 128
POOL_STATE = 15
GMLP_WIDTH = 512
N_GMLP_GROUPS = 4
GMLP_GROUP = 128
CHUNK = 128
D_FF = 2816
EPS = 1e-6

O1 = POOL_WIDTH
O2 = O1 + 2 * GMLP_WIDTH
O3 = O2 + D_MODEL
IN_WIDTH = O3 + D_MODEL

HALO = 16
ROW_TILE = 256
FFN_CHUNK = 256
N_FFN_CHUNKS = D_FF // FFN_CHUNK
VMEM_LIMIT_BYTES = 60 * 1024 * 1024

STAGE_SLOTS = 4
STAGE_LOOKAHEAD = STAGE_SLOTS - 1
STAGE_BYTES = 1 << 20

F32 = jnp.float32
BF16 = jnp.bfloat16


def _dot(a, b):
    return jnp.dot(a, b, preferred_element_type=F32)


def _rms(x, g):
    ms = jnp.mean(x * x, axis=-1, keepdims=True)
    return x * lax.rsqrt(ms + EPS) * g


def _sigmoid(x):
    return 0.5 * jnp.tanh(0.5 * x) + 0.5


def _silu(x):
    h = 0.5 * x
    return h * jnp.tanh(h) + h


def _window_sums(ext, w):
    s = ext
    k = 1
    while k < w:
        s = s + pltpu.roll(s, k, axis=0)
        k *= 2
    return s


def _pool_tile(a, t, halo, halo_ref, pool_ref):
    tm = a.shape[0]
    ext = jnp.concatenate([halo, a], axis=0)
    pos = lax.broadcasted_iota(jnp.int32, (tm, 1), 0) + t * tm
    parts = []
    for gi, w in enumerate(POOL_WINDOWS):
        lanes = slice(gi * POOL_GROUP, (gi + 1) * POOL_GROUP)
        wsum = _window_sums(ext[:, lanes], w)[HALO:]
        inv_cnt = 1.0 / jnp.minimum(w, pos + 1).astype(F32)
        parts.append(wsum * inv_cnt - a[:, lanes])
    tail = a[tm - HALO:, :]
    pool_ref[0] = tail
    halo_ref[...] = tail
    return jnp.concatenate(parts, axis=1).astype(BF16)


def _pool_project(pb, w_pool_ref, s_pool_ref):
    lo = _dot(pb[:, 0:256], w_pool_ref[0])
    hi = _dot(pb[:, 256:512], w_pool_ref[1])
    return (jnp.concatenate([lo, hi], axis=1) * s_pool_ref[...]).astype(BF16)


def _spatial_gate(vb, w_s_ref, bias_ref):
    causal = (lax.broadcasted_iota(jnp.int32, (CHUNK, CHUNK), 0)
              >= lax.broadcasted_iota(jnp.int32, (CHUNK, CHUNK), 1))
    w_causal = [jnp.where(causal, w_s_ref[g], jnp.zeros((CHUNK, CHUNK), BF16))
                for g in range(N_GMLP_GROUPS)]
    pair_w = 2 * GMLP_GROUP
    w_pairs = [jnp.concatenate(w_causal[2 * p:2 * p + 2], axis=1)
               for p in range(N_GMLP_GROUPS // 2)]
    first = lax.broadcasted_iota(jnp.int32, (CHUNK, pair_w), 1) < GMLP_GROUP
    no_v = jnp.zeros((CHUNK, pair_w), BF16)
    rows = []
    for c in range(vb.shape[0] // CHUNK):
        cols = []
        for p in range(N_GMLP_GROUPS // 2):
            v_pair = vb[c * CHUNK:(c + 1) * CHUNK, p * pair_w:(p + 1) * pair_w]
            v_diag = jnp.concatenate([jnp.where(first, v_pair, no_v),
                                      jnp.where(first, no_v, v_pair)], axis=0)
            cols.append(_dot(w_pairs[p], v_diag))
        rows.append(jnp.concatenate(cols, axis=1) + bias_ref[...])
    return jnp.concatenate(rows, axis=0)


def _mixer_first_step(x_ref, w_in_ref, g_mix_ref, w_pool_ref, s_pool_ref, w_s_ref,
                      bias_ref, g_v_ref, w_po_ref, w_go_ref, w_out_ref, g_ffn_ref,
                      pool_ref, halo_ref, x1buf, h2buf):
    x = x_ref[0]
    h = _rms(x, g_mix_ref[...]).astype(BF16)
    a = _dot(h, w_in_ref[:, 0:O1])
    z_v = _dot(h, w_in_ref[:, O1 + GMLP_WIDTH:O2])
    z_u = _dot(h, w_in_ref[:, O1:O1 + GMLP_WIDTH])
    pb = _pool_tile(a, 0, jnp.zeros((HALO, POOL_WIDTH), F32), halo_ref, pool_ref)
    gate_a = _sigmoid(_dot(h, w_in_ref[:, O2:O3]))
    vb = _rms(jax.nn.gelu(z_v), g_v_ref[...]).astype(BF16)
    gate_b = _sigmoid(_dot(h, w_in_ref[:, O3:IN_WIDTH]))
    pa = _pool_project(pb, w_pool_ref, s_pool_ref)
    sg = (jax.nn.gelu(z_u) * _spatial_gate(vb, w_s_ref, bias_ref)).astype(BF16)
    merged = (gate_a * _dot(pa, w_po_ref[...])
              + gate_b * _dot(sg, w_go_ref[...])).astype(BF16)
    x1 = x + _dot(merged, w_out_ref[...])
    x1buf[...] = x1
    h2buf[...] = _rms(x1, g_ffn_ref[...]).astype(BF16)


def _ffn_last_step(w_gate_ref, w_up_ref, w_down_ref, g_final_ref, y_ref, x1buf, h2buf):
    h2 = h2buf[...]
    act = (_silu(_dot(h2, w_gate_ref[...])) * _dot(h2, w_up_ref[...])).astype(BF16)
    x2 = x1buf[...] + _dot(act, w_down_ref[...])
    y_ref[0] = _rms(x2, g_final_ref[...])


def _steady_step(t, x_ref, w_in_ref, g_mix_ref, w_pool_ref, s_pool_ref, w_s_ref,
                 bias_ref, g_v_ref, w_po_ref, w_go_ref, w_out_ref, g_ffn_ref,
                 w_gate_ref, w_up_ref, w_down_ref, g_final_ref,
                 y_ref, pool_ref, halo_ref, x1buf, h2buf):
    tm = x_ref.shape[1]
    half = D_MODEL // 2
    quarter = D_MODEL // 4

    h2 = h2buf[...]
    acts = {}
    ffn_acc = []

    def gate_up(c):
        cols = slice(c * FFN_CHUNK, (c + 1) * FFN_CHUNK)
        acts[c] = (_silu(_dot(h2, w_gate_ref[:, cols]))
                   * _dot(h2, w_up_ref[:, cols])).astype(BF16)

    def down(c):
        part = _dot(acts.pop(c), w_down_ref[c * FFN_CHUNK:(c + 1) * FFN_CHUNK, :])
        ffn_acc.append(part + (ffn_acc[-1] if ffn_acc else x1buf[...]))

    def in_gate(base, i):
        return _sigmoid(_dot(h, w_in_ref[:, base + i * quarter:base + (i + 1) * quarter]))

    x = x_ref[0]
    gate_up(0)
    h = _rms(x, g_mix_ref[...]).astype(BF16)
    gate_up(1)
    a = _dot(h, w_in_ref[:, 0:O1])
    down(0)
    gate_up(2)
    z_v = _dot(h, w_in_ref[:, O1 + GMLP_WIDTH:O2])
    down(1)
    gate_up(3)
    pb = _pool_tile(a, t, jnp.where(t == 0, 0.0, halo_ref[...]), halo_ref, pool_ref)
    z_u = _dot(h, w_in_ref[:, O1:O1 + GMLP_WIDTH])
    down(2)
    gate_up(4)
    vb = _rms(jax.nn.gelu(z_v), g_v_ref[...]).astype(BF16)
    pa = _pool_project(pb, w_pool_ref, s_pool_ref)
    gate_a = [in_gate(O2, 0), in_gate(O2, 1)]
    down(3)
    gate_up(5)
    u = jax.nn.gelu(z_u)
    s_gate = _spatial_gate(vb, w_s_ref, bias_ref)
    gate_a += [in_gate(O2, 2), in_gate(O2, 3)]
    down(4)
    gate_up(6)
    sg = (u * s_gate).astype(BF16)
    gate_b = [in_gate(O3, 0), in_gate(O3, 1)]
    down(5)
    gate_up(7)
    gate_b += [in_gate(O3, 2), in_gate(O3, 3)]
    po_lo = _dot(pa, w_po_ref[:, 0:half])
    go_lo = _dot(sg, w_go_ref[:, 0:half])
    down(6)
    gate_up(8)
    merged_lo = (jnp.concatenate(gate_a[0:2], axis=1) * po_lo
                 + jnp.concatenate(gate_b[0:2], axis=1) * go_lo).astype(BF16)
    po_hi = _dot(pa, w_po_ref[:, half:])
    go_hi = _dot(sg, w_go_ref[:, half:])
    down(7)
    gate_up(9)
    merged_hi = (jnp.concatenate(gate_a[2:4], axis=1) * po_hi
                 + jnp.concatenate(gate_b[2:4], axis=1) * go_hi).astype(BF16)
    out_lo = _dot(merged_lo, w_out_ref[0:half, :])
    down(8)
    gate_up(10)
    x1 = x + out_lo + _dot(merged_hi, w_out_ref[half:, :])
    down(9)
    x1buf[...] = x1
    h2buf[...] = _rms(x1, g_ffn_ref[...]).astype(BF16)
    down(10)
    y_ref[0] = _rms(ffn_acc[-1], g_final_ref[...])


def _load_weights(jobs, *scoped):
    widths = sorted({dst.shape[1] for _, dst in jobs})
    stages = dict(zip(widths, scoped[:-1]))
    sems = scoped[-1]
    plan = []
    used = {c: 0 for c in widths}
    for src, dst in jobs:
        n_rows, c = dst.shape
        rows = stages[c].shape[1]
        for r0 in range(0, n_rows, rows):
            plan.append((src, dst, r0, min(rows, n_rows - r0), c, used[c] % STAGE_SLOTS))
            used[c] += 1

    def copy(i):
        src, _, r0, rr, c, slot = plan[i]
        return pltpu.make_async_copy(src.at[0, pl.ds(r0, rr), :],
                                     stages[c].at[slot, pl.ds(0, rr), :],
                                     sems.at[i % STAGE_SLOTS])

    for i in range(min(STAGE_LOOKAHEAD, len(plan))):
        copy(i).start()
    for i, (_, dst, r0, rr, c, slot) in enumerate(plan):
        if i + STAGE_LOOKAHEAD < len(plan):
            copy(i + STAGE_LOOKAHEAD).start()
        copy(i).wait()
        dst[pl.ds(r0, rr), :] = stages[c][slot, 0:rr, :].astype(BF16)


def _stage_rows(width):
    rows = STAGE_BYTES // (4 * width)
    return max(16, rows - rows % 16)


def _sample_rows(xs_ref, state_hbm, pool_s_hbm, w_in_ref, g_mix_ref, w_pool_ref,
                 s_pool_ref, w_diag_ref, bias0_ref, g_v_ref, w_po_ref, w_go_ref,
                 w_out_ref, g_ffn_ref, w_gate_ref, w_up_ref, w_down_ref, g_final_ref,
                 ys_ref, vs_ref, before_compute, state_buf, a_buf, sems):
    load_state = pltpu.make_async_copy(state_hbm, state_buf, sems.at[0])
    shift_state = pltpu.make_async_copy(state_hbm.at[pl.ds(1, POOL_STATE - 1)],
                                        pool_s_hbm.at[pl.ds(0, POOL_STATE - 1)],
                                        sems.at[1])
    store_row = pltpu.make_async_copy(a_buf, pool_s_hbm.at[POOL_STATE - 1], sems.at[2])
    load_state.start()
    shift_state.start()
    before_compute()

    x = xs_ref[...]
    h = _rms(x, g_mix_ref[...]).astype(BF16)
    a = _dot(h, w_in_ref[:, 0:O1])
    a_buf[...] = a
    store_row.start()
    z = jax.nn.gelu(_dot(h, w_in_ref[:, O1:O2]))
    gate_a = _sigmoid(_dot(h, w_in_ref[:, O2:O3]))
    gate_b = _sigmoid(_dot(h, w_in_ref[:, O3:IN_WIDTH]))

    load_state.wait()
    parts = []
    for gi, w in enumerate(POOL_WINDOWS):
        lanes = slice(gi * POOL_GROUP, (gi + 1) * POOL_GROUP)
        xg = a[:, lanes]
        wsum = xg
        for k in range(1, w):
            wsum = wsum + state_buf[POOL_STATE - k, :, lanes]
        parts.append(wsum * (1.0 / w) - xg)
    pa = _pool_project(jnp.concatenate(parts, axis=1).astype(BF16), w_pool_ref, s_pool_ref)

    u = z[:, 0:GMLP_WIDTH]
    v = _rms(z[:, GMLP_WIDTH:], g_v_ref[...])
    vs_ref[...] = v
    sg = (u * (w_diag_ref[...] * v + bias0_ref[...])).astype(BF16)

    merged = (gate_a * _dot(pa, w_po_ref[...])
              + gate_b * _dot(sg, w_go_ref[...])).astype(BF16)
    x1 = x + _dot(merged, w_out_ref[...])
    h2 = _rms(x1, g_ffn_ref[...]).astype(BF16)
    act = (_silu(_dot(h2, w_gate_ref[...])) * _dot(h2, w_up_ref[...])).astype(BF16)
    x2 = x1 + _dot(act, w_down_ref[...])
    ys_ref[...] = _rms(x2, g_final_ref[...])
    shift_state.wait()
    store_row.wait()


def _layer_kernel(x_ref, xs_ref, state_hbm, w_in_hbm, w_po_hbm, w_go_hbm, w_out_hbm,
                  w_gate_hbm, w_up_hbm, w_down_hbm, g_mix_ref, w_pool_ref, s_pool_ref,
                  w_s_ref, bias_ref, g_v_ref, g_ffn_ref, g_final_ref, w_diag_ref,
                  bias0_ref,
                  y_ref, pool_ref, ys_ref, vs_ref, pool_s_hbm,
                  w_in_ref, w_po_ref, w_go_ref, w_out_ref, w_gate_ref, w_up_ref,
                  w_down_ref, halo_ref, x1buf, h2buf, *, n_tiles, tiles_per_seq):
    s = pl.program_id(0)
    mixer_refs = (x_ref, w_in_ref, g_mix_ref, w_pool_ref, s_pool_ref, w_s_ref, bias_ref,
                  g_v_ref, w_po_ref, w_go_ref, w_out_ref, g_ffn_ref)
    ffn_refs = (w_gate_ref, w_up_ref, w_down_ref, g_final_ref)
    carry = (halo_ref, x1buf, h2buf)

    @pl.when(s == 0)
    def _():
        jobs = ((w_in_hbm, w_in_ref), (w_po_hbm, w_po_ref), (w_go_hbm, w_go_ref),
                (w_out_hbm, w_out_ref), (w_gate_hbm, w_gate_ref), (w_up_hbm, w_up_ref),
                (w_down_hbm, w_down_ref))
        widths = sorted({dst.shape[1] for _, dst in jobs})
        pl.run_scoped(
            functools.partial(_load_weights, jobs),
            *[pltpu.VMEM((STAGE_SLOTS, _stage_rows(c), c), F32) for c in widths],
            pltpu.SemaphoreType.DMA((STAGE_SLOTS,)))
        pl.run_scoped(
            functools.partial(
                _sample_rows, xs_ref, state_hbm, pool_s_hbm, w_in_ref, g_mix_ref,
                w_pool_ref, s_pool_ref, w_diag_ref, bias0_ref, g_v_ref, w_po_ref,
                w_go_ref, w_out_ref, g_ffn_ref, *ffn_refs, ys_ref, vs_ref,
                functools.partial(_mixer_first_step, *mixer_refs, pool_ref, *carry)),
            pltpu.VMEM(state_hbm.shape, F32),
            pltpu.VMEM((state_hbm.shape[1], POOL_WIDTH), F32),
            pltpu.SemaphoreType.DMA((3,)))

    @pl.when(jnp.logical_and(s > 0, s < n_tiles))
    def _():
        _steady_step(lax.rem(s, tiles_per_seq), *mixer_refs, *ffn_refs,
                     y_ref, pool_ref, *carry)

    @pl.when(s == n_tiles)
    def _():
        _ffn_last_step(*ffn_refs, y_ref, x1buf, h2buf)


def _resident(shape):
    zeros = (0,) * len(shape)
    return pl.BlockSpec(shape, lambda s: zeros, pipeline_mode=pl.Buffered(1))


def kernel(x_prompt, x_sample, state_pool, w_in, g_mix, w_pool, s_pool, w_s, b_s, g_v,
           w_pool_out, w_gmlp_out, w_out, g_ffn, w_gate, w_up, w_down, g_final):
    batch, seq, _ = x_prompt.shape
    dec_batch = x_sample.shape[0]
    assert w_in.shape[0] == 1 and x_sample.shape[1] == 1
    assert seq % ROW_TILE == 0 and ROW_TILE % CHUNK == 0
    assert N_FFN_CHUNKS == 11

    zero_blk = jnp.zeros((POOL_GROUP, POOL_GROUP), F32)
    wp = w_pool[0]
    w_pool_b = jnp.stack([
        jnp.block([[wp[0], zero_blk], [zero_blk, wp[1]]]),
        jnp.block([[wp[2], zero_blk], [zero_blk, wp[3]]]),
    ]).astype(BF16)
    w_s_b = w_s[0].astype(BF16)
    g_mix2 = g_mix[0][None, :]
    s_pool2 = s_pool[0][None, :]
    g_v2 = g_v[0][None, :]
    g_ffn2 = g_ffn[0][None, :]
    g_final2 = g_final[None, :]
    bias_full = jnp.repeat(jnp.transpose(b_s[0]), GMLP_GROUP, axis=1)
    w_diag = jnp.repeat(w_s[0][:, 0, 0], GMLP_GROUP)[None, :]
    bias0 = bias_full[0:1, :]
    state_t = jnp.transpose(state_pool[0], (1, 0, 2))

    tiles_per_seq = seq // ROW_TILE
    n_tiles = batch * tiles_per_seq

    def mixer_tile(s):
        i = jnp.minimum(s, n_tiles - 1)
        return i // tiles_per_seq, i % tiles_per_seq

    def ffn_tile(s):
        i = jnp.maximum(s - 1, 0)
        return i // tiles_per_seq, i % tiles_per_seq

    big_weights = (w_in, w_pool_out, w_gmlp_out, w_out, w_gate, w_up, w_down)
    small = (g_mix2, w_pool_b, s_pool2, w_s_b, bias_full, g_v2, g_ffn2, g_final2,
             w_diag, bias0)
    x_s = x_sample[:, 0, :]
    in_specs = [pl.BlockSpec((1, ROW_TILE, D_MODEL), lambda s: (*mixer_tile(s), 0)),
                _resident(x_s.shape)]
    in_specs += [pl.BlockSpec(memory_space=pl.ANY)] * (1 + len(big_weights))
    in_specs += [_resident(arr.shape) for arr in small]
    y_prompt, pool_p, y_s, v_s, pool_s = pl.pallas_call(
        functools.partial(_layer_kernel, n_tiles=n_tiles, tiles_per_seq=tiles_per_seq),
        grid=(n_tiles + 1,),
        in_specs=in_specs,
        out_specs=[
            pl.BlockSpec((1, ROW_TILE, D_MODEL), lambda s: (*ffn_tile(s), 0)),
            pl.BlockSpec((1, HALO, POOL_WIDTH), lambda s: (mixer_tile(s)[0], 0, 0)),
            pl.BlockSpec((dec_batch, D_MODEL), lambda s: (0, 0)),
            pl.BlockSpec((dec_batch, GMLP_WIDTH), lambda s: (0, 0)),
            pl.BlockSpec(memory_space=pl.ANY),
        ],
        out_shape=[
            jax.ShapeDtypeStruct((batch, seq, D_MODEL), F32),
            jax.ShapeDtypeStruct((batch, HALO, POOL_WIDTH), F32),
            jax.ShapeDtypeStruct((dec_batch, D_MODEL), F32),
            jax.ShapeDtypeStruct((dec_batch, GMLP_WIDTH), F32),
            jax.ShapeDtypeStruct((POOL_STATE, dec_batch, POOL_WIDTH), F32),
        ],
        scratch_shapes=[pltpu.VMEM(w.shape[1:], BF16) for w in big_weights] + [
            pltpu.VMEM((HALO, POOL_WIDTH), F32),
            pltpu.VMEM((ROW_TILE, D_MODEL), F32),
            pltpu.VMEM((ROW_TILE, D_MODEL), BF16),
        ],
        compiler_params=pltpu.CompilerParams(
            dimension_semantics=("arbitrary",),
            vmem_limit_bytes=VMEM_LIMIT_BYTES),
        name="layer",
    )(x_prompt, x_s, state_t, *big_weights, *small)

    return (
        y_prompt,
        y_s[:, None, :],
        pool_p[None, :, HALO - POOL_STATE:, :],
        jnp.transpose(pool_s, (1, 0, 2))[None],
        v_s[None, :, None, :],
    )
```

```python
import functools

import jax
import jax.numpy as jnp
from jax import lax
from jax.experimental import pallas as pl
from jax.experimental.pallas import tpu as pltpu

D_MODEL = 1024
POOL_WINDOWS = (2, 4, 8, 16)
POOL_WIDTH = 512
POOL_GROUP = 128
POOL_STATE = 15
GMLP_WIDTH = 512
N_GMLP_GROUPS = 4
GMLP_GROUP = 128
CHUNK = 128
D_FF = 2816
EPS = 1e-6

O1 = POOL_WIDTH
O2 = O1 + 2 * GMLP_WIDTH
O3 = O2 + D_MODEL
IN_WIDTH = O3 + D_MODEL

HALO = 16
ROW_TILE = 256
FFN_CHUNK = 256
N_FFN_CHUNKS = D_FF // FFN_CHUNK
VMEM_LIMIT_BYTES = 60 * 1024 * 1024

STAGE_SLOTS = 4
STAGE_LOOKAHEAD = STAGE_SLOTS - 1
STAGE_BYTES = 1 << 20

F32 = jnp.float32
BF16 = jnp.bfloat16


def _dot(a, b):
    return jnp.dot(a, b, preferred_element_type=F32)


def _rms(x, g):
    ms = jnp.mean(x * x, axis=-1, keepdims=True)
    return x * lax.rsqrt(ms + EPS) * g


def _sigmoid(x):
    return 0.5 * jnp.tanh(0.5 * x) + 0.5


def _silu(x):
    h = 0.5 * x
    return h * jnp.tanh(h) + h


def _window_sums(ext, w):
    s = ext
    k = 1
    while k < w:
        s = s + pltpu.roll(s, k, axis=0)
        k *= 2
    return s


def _pool_tile(a, t, halo, halo_ref, pool_ref):
    tm = a.shape[0]
    ext = jnp.concatenate([halo, a], axis=0)
    pos = lax.broadcasted_iota(jnp.int32, (tm, 1), 0) + t * tm
    parts = []
    for gi, w in enumerate(POOL_WINDOWS):
        lanes = slice(gi * POOL_GROUP, (gi + 1) * POOL_GROUP)
        wsum = _window_sums(ext[:, lanes], w)[HALO:]
        inv_cnt = 1.0 / jnp.minimum(w, pos + 1).astype(F32)
        parts.append(wsum * inv_cnt - a[:, lanes])
    tail = a[tm - HALO:, :]
    pool_ref[0] = tail
    halo_ref[...] = tail
    return jnp.concatenate(parts, axis=1).astype(BF16)


def _pool_project(pb, w_pool_ref, s_pool_ref):
    lo = _dot(pb[:, 0:256], w_pool_ref[0])
    hi = _dot(pb[:, 256:512], w_pool_ref[1])
    return (jnp.concatenate([lo, hi], axis=1) * s_pool_ref[...]).astype(BF16)


def _spatial_gate(vb, w_s_ref, bias_ref):
    causal = (lax.broadcasted_iota(jnp.int32, (CHUNK, CHUNK), 0)
              >= lax.broadcasted_iota(jnp.int32, (CHUNK, CHUNK), 1))
    w_causal = [jnp.where(causal, w_s_ref[g], jnp.zeros((CHUNK, CHUNK), BF16))
                for g in range(N_GMLP_GROUPS)]
    pair_w = 2 * GMLP_GROUP
    w_pairs = [jnp.concatenate(w_causal[2 * p:2 * p + 2], axis=1)
               for p in range(N_GMLP_GROUPS // 2)]
    first = lax.broadcasted_iota(jnp.int32, (CHUNK, pair_w), 1) < GMLP_GROUP
    no_v = jnp.zeros((CHUNK, pair_w), BF16)
    rows = []
    for c in range(vb.shape[0] // CHUNK):
        cols = []
        for p in range(N_GMLP_GROUPS // 2):
            v_pair = vb[c * CHUNK:(c + 1) * CHUNK, p * pair_w:(p + 1) * pair_w]
            v_diag = jnp.concatenate([jnp.where(first, v_pair, no_v),
                                      jnp.where(first, no_v, v_pair)], axis=0)
            cols.append(_dot(w_pairs[p], v_diag))
        rows.append(jnp.concatenate(cols, axis=1) + bias_ref[...])
    return jnp.concatenate(rows, axis=0)


def _mixer_first_step(x_ref, w_in_ref, g_mix_ref, w_pool_ref, s_pool_ref, w_s_ref,
                      bias_ref, g_v_ref, w_po_ref, w_go_ref, w_out_ref, g_ffn_ref,
                      pool_ref, halo_ref, x1buf, h2buf):
    x = x_ref[0]
    h = _rms(x, g_mix_ref[...]).astype(BF16)
    a = _dot(h, w_in_ref[:, 0:O1])
    z_v = _dot(h, w_in_ref[:, O1 + GMLP_WIDTH:O2])
    z_u = _dot(h, w_in_ref[:, O1:O1 + GMLP_WIDTH])
    pb = _pool_tile(a, 0, jnp.zeros((HALO, POOL_WIDTH), F32), halo_ref, pool_ref)
    gate_a = _sigmoid(_dot(h, w_in_ref[:, O2:O3]))
    vb = _rms(jax.nn.gelu(z_v), g_v_ref[...]).astype(BF16)
    gate_b = _sigmoid(_dot(h, w_in_ref[:, O3:IN_WIDTH]))
    pa = _pool_project(pb, w_pool_ref, s_pool_ref)
    sg = (jax.nn.gelu(z_u) * _spatial_gate(vb, w_s_ref, bias_ref)).astype(BF16)
    merged = (gate_a * _dot(pa, w_po_ref[...])
              + gate_b * _dot(sg, w_go_ref[...])).astype(BF16)
    x1 = x + _dot(merged, w_out_ref[...])
    x1buf[...] = x1
    h2buf[...] = _rms(x1, g_ffn_ref[...]).astype(BF16)


def _ffn_last_step(w_gate_ref, w_up_ref, w_down_ref, g_final_ref, y_ref, x1buf, h2buf):
    h2 = h2buf[...]
    act = (_silu(_dot(h2, w_gate_ref[...])) * _dot(h2, w_up_ref[...])).astype(BF16)
    x2 = x1buf[...] + _dot(act, w_down_ref[...])
    y_ref[0] = _rms(x2, g_final_ref[...])


def _steady_step(t, x_ref, w_in_ref, g_mix_ref, w_pool_ref, s_pool_ref, w_s_ref,
                 bias_ref, g_v_ref, w_po_ref, w_go_ref, w_out_ref, g_ffn_ref,
                 w_gate_ref, w_up_ref, w_down_ref, g_final_ref,
                 y_ref, pool_ref, halo_ref, x1buf, h2buf):
    tm = x_ref.shape[1]
    half = D_MODEL // 2
    quarter = D_MODEL // 4

    h2 = h2buf[...]
    acts = {}
    ffn_acc = []

    def gate_up(c):
        cols = slice(c * FFN_CHUNK, (c + 1) * FFN_CHUNK)
        acts[c] = (_silu(_dot(h2, w_gate_ref[:, cols]))
                   * _dot(h2, w_up_ref[:, cols])).astype(BF16)

    def down(c):
        part = _dot(acts.pop(c), w_down_ref[c * FFN_CHUNK:(c + 1) * FFN_CHUNK, :])
        ffn_acc.append(part + (ffn_acc[-1] if ffn_acc else x1buf[...]))

    def in_gate(base, i):
        return _sigmoid(_dot(h, w_in_ref[:, base + i * quarter:base + (i + 1) * quarter]))

    x = x_ref[0]
    gate_up(0)
    h = _rms(x, g_mix_ref[...]).astype(BF16)
    gate_up(1)
    a = _dot(h, w_in_ref[:, 0:O1])
    down(0)
    gate_up(2)
    z_v = _dot(h, w_in_ref[:, O1 + GMLP_WIDTH:O2])
    down(1)
    gate_up(3)
    pb = _pool_tile(a, t, jnp.where(t == 0, 0.0, halo_ref[...]), halo_ref, pool_ref)
    z_u = _dot(h, w_in_ref[:, O1:O1 + GMLP_WIDTH])
    down(2)
    gate_up(4)
    vb = _rms(jax.nn.gelu(z_v), g_v_ref[...]).astype(BF16)
    pa = _pool_project(pb, w_pool_ref, s_pool_ref)
    gate_a = [in_gate(O2, 0), in_gate(O2, 1)]
    down(3)
    gate_up(5)
    u = jax.nn.gelu(z_u)
    s_gate = _spatial_gate(vb, w_s_ref, bias_ref)
    gate_a += [in_gate(O2, 2), in_gate(O2, 3)]
    down(4)
    gate_up(6)
    sg = (u * s_gate).astype(BF16)
    gate_b = [in_gate(O3, 0), in_gate(O3, 1)]
    down(5)
    gate_up(7)
    gate_b += [in_gate(O3, 2), in_gate(O3, 3)]
    po_lo = _dot(pa, w_po_ref[:, 0:half])
    go_lo = _dot(sg, w_go_ref[:, 0:half])
    down(6)
    gate_up(8)
    merged_lo = (jnp.concatenate(gate_a[0:2], axis=1) * po_lo
                 + jnp.concatenate(gate_b[0:2], axis=1) * go_lo).astype(BF16)
    po_hi = _dot(pa, w_po_ref[:, half:])
    go_hi = _dot(sg, w_go_ref[:, half:])
    down(7)
    gate_up(9)
    merged_hi = (jnp.concatenate(gate_a[2:4], axis=1) * po_hi
                 + jnp.concatenate(gate_b[2:4], axis=1) * go_hi).astype(BF16)
    out_lo = _dot(merged_lo, w_out_ref[0:half, :])
    down(8)
    gate_up(10)
    x1 = x + out_lo + _dot(merged_hi, w_out_ref[half:, :])
    down(9)
    x1buf[...] = x1
    h2buf[...] = _rms(x1, g_ffn_ref[...]).astype(BF16)
    down(10)
    y_ref[0] = _rms(ffn_acc[-1], g_final_ref[...])


def _load_weights(jobs, *scoped):
    widths = sorted({dst.shape[1] for _, dst in jobs})
    stages = dict(zip(widths, scoped[:-1]))
    sems = scoped[-1]
    plan = []
    used = {c: 0 for c in widths}
    for src, dst in jobs:
        n_rows, c = dst.shape
        rows = stages[c].shape[1]
        for r0 in range(0, n_rows, rows):
            plan.append((src, dst, r0, min(rows, n_rows - r0), c, used[c] % STAGE_SLOTS))
            used[c] += 1

    def copy(i):
        src, _, r0, rr, c, slot = plan[i]
        return pltpu.make_async_copy(src.at[0, pl.ds(r0, rr), :],
                                     stages[c].at[slot, pl.ds(0, rr), :],
                                     sems.at[i % STAGE_SLOTS])

    for i in range(min(STAGE_LOOKAHEAD, len(plan))):
        copy(i).start()
    for i, (_, dst, r0, rr, c, slot) in enumerate(plan):
        if i + STAGE_LOOKAHEAD < len(plan):
            copy(i + STAGE_LOOKAHEAD).start()
        copy(i).wait()
        dst[pl.ds(r0, rr), :] = stages[c][slot, 0:rr, :].astype(BF16)


def _stage_rows(width):
    rows = STAGE_BYTES // (4 * width)
    return max(16, rows - rows % 16)


def _sample_rows(xs_ref, state_hbm, pool_s_hbm, w_in_ref, g_mix_ref, w_pool_ref,
                 s_pool_ref, w_diag_ref, bias0_ref, g_v_ref, w_po_ref, w_go_ref,
                 w_out_ref, g_ffn_ref, w_gate_ref, w_up_ref, w_down_ref, g_final_ref,
                 ys_ref, vs_ref, before_compute, state_buf, a_buf, sems):
    load_state = pltpu.make_async_copy(state_hbm, state_buf, sems.at[0])
    shift_state = pltpu.make_async_copy(state_buf.at[pl.ds(1, POOL_STATE - 1)],
                                        pool_s_hbm.at[pl.ds(0, POOL_STATE - 1)],
                                        sems.at[1])
    store_row = pltpu.make_async_copy(a_buf, pool_s_hbm.at[POOL_STATE - 1], sems.at[2])
    load_state.start()
    before_compute()

    x = xs_ref[...]
    h = _rms(x, g_mix_ref[...]).astype(BF16)
    a = _dot(h, w_in_ref[:, 0:O1])
    a_buf[...] = a
    store_row.start()
    z = jax.nn.gelu(_dot(h, w_in_ref[:, O1:O2]))
    gate_a = _sigmoid(_dot(h, w_in_ref[:, O2:O3]))
    gate_b = _sigmoid(_dot(h, w_in_ref[:, O3:IN_WIDTH]))

    load_state.wait()
    shift_state.start()
    parts = []
    for gi, w in enumerate(POOL_WINDOWS):
        lanes = slice(gi * POOL_GROUP, (gi + 1) * POOL_GROUP)
        xg = a[:, lanes]
        wsum = xg
        for k in range(1, w):
            wsum = wsum + state_buf[POOL_STATE - k, :, lanes]
        parts.append(wsum * (1.0 / w) - xg)
    pa = _pool_project(jnp.concatenate(parts, axis=1).astype(BF16), w_pool_ref, s_pool_ref)

    u = z[:, 0:GMLP_WIDTH]
    v = _rms(z[:, GMLP_WIDTH:], g_v_ref[...])
    vs_ref[...] = v
    sg = (u * (w_diag_ref[...] * v + bias0_ref[...])).astype(BF16)

    merged = (gate_a * _dot(pa, w_po_ref[...])
              + gate_b * _dot(sg, w_go_ref[...])).astype(BF16)
    x1 = x + _dot(merged, w_out_ref[...])
    h2 = _rms(x1, g_ffn_ref[...]).astype(BF16)
    act = (_silu(_dot(h2, w_gate_ref[...])) * _dot(h2, w_up_ref[...])).astype(BF16)
    x2 = x1 + _dot(act, w_down_ref[...])
    ys_ref[...] = _rms(x2, g_final_ref[...])
    shift_state.wait()
    store_row.wait()


def _layer_kernel(x_ref, xs_ref, state_hbm, w_in_hbm, w_po_hbm, w_go_hbm, w_out_hbm,
                  w_gate_hbm, w_up_hbm, w_down_hbm, g_mix_ref, w_pool_ref, s_pool_ref,
                  w_s_ref, bias_ref, g_v_ref, g_ffn_ref, g_final_ref, w_diag_ref,
                  bias0_ref,
                  y_ref, pool_ref, ys_ref, vs_ref, pool_s_hbm,
                  w_in_ref, w_po_ref, w_go_ref, w_out_ref, w_gate_ref, w_up_ref,
                  w_down_ref, halo_ref, x1buf, h2buf, *, n_tiles, tiles_per_seq):
    s = pl.program_id(0)
    mixer_refs = (x_ref, w_in_ref, g_mix_ref, w_pool_ref, s_pool_ref, w_s_ref, bias_ref,
                  g_v_ref, w_po_ref, w_go_ref, w_out_ref, g_ffn_ref)
    ffn_refs = (w_gate_ref, w_up_ref, w_down_ref, g_final_ref)
    carry = (halo_ref, x1buf, h2buf)

    @pl.when(s == 0)
    def _():
        jobs = ((w_in_hbm, w_in_ref), (w_po_hbm, w_po_ref), (w_go_hbm, w_go_ref),
                (w_out_hbm, w_out_ref), (w_gate_hbm, w_gate_ref), (w_up_hbm, w_up_ref),
                (w_down_hbm, w_down_ref))
        widths = sorted({dst.shape[1] for _, dst in jobs})
        pl.run_scoped(
            functools.partial(_load_weights, jobs),
            *[pltpu.VMEM((STAGE_SLOTS, _stage_rows(c), c), F32) for c in widths],
            pltpu.SemaphoreType.DMA((STAGE_SLOTS,)))
        pl.run_scoped(
            functools.partial(
                _sample_rows, xs_ref, state_hbm, pool_s_hbm, w_in_ref, g_mix_ref,
                w_pool_ref, s_pool_ref, w_diag_ref, bias0_ref, g_v_ref, w_po_ref,
                w_go_ref, w_out_ref, g_ffn_ref, *ffn_refs, ys_ref, vs_ref,
                functools.partial(_mixer_first_step, *mixer_refs, pool_ref, *carry)),
            pltpu.VMEM(state_hbm.shape, F32),
            pltpu.VMEM((state_hbm.shape[1], POOL_WIDTH), F32),
            pltpu.SemaphoreType.DMA((3,)))

    @pl.when(jnp.logical_and(s > 0, s < n_tiles))
    def _():
        _steady_step(lax.rem(s, tiles_per_seq), *mixer_refs, *ffn_refs,
                     y_ref, pool_ref, *carry)

    @pl.when(s == n_tiles)
    def _():
        _ffn_last_step(*ffn_refs, y_ref, x1buf, h2buf)


def _resident(shape):
    zeros = (0,) * len(shape)
    return pl.BlockSpec(shape, lambda s: zeros, pipeline_mode=pl.Buffered(1))


def kernel(x_prompt, x_sample, state_pool, w_in, g_mix, w_pool, s_pool, w_s, b_s, g_v,
           w_pool_out, w_gmlp_out, w_out, g_ffn, w_gate, w_up, w_down, g_final):
    batch, seq, _ = x_prompt.shape
    dec_batch = x_sample.shape[0]
    assert w_in.shape[0] == 1 and x_sample.shape[1] == 1
    assert seq % ROW_TILE == 0 and ROW_TILE % CHUNK == 0
    assert N_FFN_CHUNKS == 11

    zero_blk = jnp.zeros((POOL_GROUP, POOL_GROUP), F32)
    wp = w_pool[0]
    w_pool_b = jnp.stack([
        jnp.block([[wp[0], zero_blk], [zero_blk, wp[1]]]),
        jnp.block([[wp[2], zero_blk], [zero_blk, wp[3]]]),
    ]).astype(BF16)
    w_s_b = w_s[0].astype(BF16)
    g_mix2 = g_mix[0][None, :]
    s_pool2 = s_pool[0][None, :]
    g_v2 = g_v[0][None, :]
    g_ffn2 = g_ffn[0][None, :]
    g_final2 = g_final[None, :]
    bias_full = jnp.repeat(jnp.transpose(b_s[0]), GMLP_GROUP, axis=1)
    w_diag = jnp.repeat(w_s[0][:, 0, 0], GMLP_GROUP)[None, :]
    bias0 = bias_full[0:1, :]
    state_t = jnp.transpose(state_pool[0], (1, 0, 2))

    tiles_per_seq = seq // ROW_TILE
    n_tiles = batch * tiles_per_seq

    def mixer_tile(s):
        i = jnp.minimum(s, n_tiles - 1)
        return i // tiles_per_seq, i % tiles_per_seq

    def ffn_tile(s):
        i = jnp.maximum(s - 1, 0)
        return i // tiles_per_seq, i % tiles_per_seq

    big_weights = (w_in, w_pool_out, w_gmlp_out, w_out, w_gate, w_up, w_down)
    small = (g_mix2, w_pool_b, s_pool2, w_s_b, bias_full, g_v2, g_ffn2, g_final2,
             w_diag, bias0)
    x_s = x_sample[:, 0, :]
    in_specs = [pl.BlockSpec((1, ROW_TILE, D_MODEL), lambda s: (*mixer_tile(s), 0)),
                _resident(x_s.shape)]
    in_specs += [pl.BlockSpec(memory_space=pl.ANY)] * (1 + len(big_weights))
    in_specs += [_resident(arr.shape) for arr in small]
    y_prompt, pool_p, y_s, v_s, pool_s = pl.pallas_call(
        functools.partial(_layer_kernel, n_tiles=n_tiles, tiles_per_seq=tiles_per_seq),
        grid=(n_tiles + 1,),
        in_specs=in_specs,
        out_specs=[
            pl.BlockSpec((1, ROW_TILE, D_MODEL), lambda s: (*ffn_tile(s), 0)),
            pl.BlockSpec((1, HALO, POOL_WIDTH), lambda s: (mixer_tile(s)[0], 0, 0)),
            pl.BlockSpec((dec_batch, D_MODEL), lambda s: (0, 0)),
            pl.BlockSpec((dec_batch, GMLP_WIDTH), lambda s: (0, 0)),
            pl.BlockSpec(memory_space=pl.ANY),
        ],
        out_shape=[
            jax.ShapeDtypeStruct((batch, seq, D_MODEL), F32),
            jax.ShapeDtypeStruct((batch, HALO, POOL_WIDTH), F32),
            jax.ShapeDtypeStruct((dec_batch, D_MODEL), F32),
            jax.ShapeDtypeStruct((dec_batch, GMLP_WIDTH), F32),
            jax.ShapeDtypeStruct((POOL_STATE, dec_batch, POOL_WIDTH), F32),
        ],
        scratch_shapes=[pltpu.VMEM(w.shape[1:], BF16) for w in big_weights] + [
            pltpu.VMEM((HALO, POOL_WIDTH), F32),
            pltpu.VMEM((ROW_TILE, D_MODEL), F32),
            pltpu.VMEM((ROW_TILE, D_MODEL), BF16),
        ],
        compiler_params=pltpu.CompilerParams(
            dimension_semantics=("arbitrary",),
            vmem_limit_bytes=VMEM_LIMIT_BYTES),
        name="layer",
    )(x_prompt, x_s, state_t, *big_weights, *small)

    return (
        y_prompt,
        y_s[:, None, :],
        pool_p[None, :, HALO - POOL_STATE:, :],
        jnp.transpose(pool_s, (1, 0, 2))[None],
        v_s[None, :, None, :],
    )
```

```python
import functools

import jax
import jax.numpy as jnp
from jax import lax
from jax.experimental import pallas as pl
from jax.experimental.pallas import tpu as pltpu

D_MODEL = 1024
POOL_WINDOWS = (2, 4, 8, 16)
POOL_WIDTH = 512
POOL_GROUP = 128
POOL_STATE = 15
GMLP_WIDTH = 512
N_GMLP_GROUPS = 4
GMLP_GROUP = 128
CHUNK = 128
D_FF = 2816
EPS = 1e-6

O1 = POOL_WIDTH
O2 = O1 + 2 * GMLP_WIDTH
O3 = O2 + D_MODEL
IN_WIDTH = O3 + D_MODEL

HALO = 16
ROW_TILE = 256
FFN_CHUNK = 256
N_FFN_CHUNKS = D_FF // FFN_CHUNK
VMEM_LIMIT_BYTES = 60 * 1024 * 1024

STAGE_SLOTS = 4
STAGE_LOOKAHEAD = STAGE_SLOTS - 1
STAGE_BYTES = 1 << 20

F32 = jnp.float32
BF16 = jnp.bfloat16


def _dot(a, b):
    return jnp.dot(a, b, preferred_element_type=F32)


def _rms(x, g):
    ms = jnp.mean(x * x, axis=-1, keepdims=True)
    return x * lax.rsqrt(ms + EPS) * g


def _sigmoid(x):
    return 0.5 * jnp.tanh(0.5 * x) + 0.5


def _silu(x):
    h = 0.5 * x
    return h * jnp.tanh(h) + h


def _window_sums(ext, w):
    s = ext
    k = 1
    while k < w:
        s = s + pltpu.roll(s, k, axis=0)
        k *= 2
    return s


def _pool_tile(a, b, t, halo, halo_ref, pool_ref):
    tm = a.shape[0]
    ext = jnp.concatenate([halo, a], axis=0)
    pos = lax.broadcasted_iota(jnp.int32, (tm, 1), 0) + t * tm
    parts = []
    for gi, w in enumerate(POOL_WINDOWS):
        lanes = slice(gi * POOL_GROUP, (gi + 1) * POOL_GROUP)
        wsum = _window_sums(ext[:, lanes], w)[HALO:]
        inv_cnt = 1.0 / jnp.minimum(w, pos + 1).astype(F32)
        parts.append(wsum * inv_cnt - a[:, lanes])
    tail = a[tm - HALO:, :]
    for j in range(POOL_STATE):
        row = HALO - POOL_STATE + j
        pool_ref[j, pl.ds(b, 1), :] = tail[row:row + 1, :]
    halo_ref[...] = tail
    return jnp.concatenate(parts, axis=1).astype(BF16)


def _pool_project(pb, w_pool_ref, s_pool_ref):
    lo = _dot(pb[:, 0:256], w_pool_ref[0])
    hi = _dot(pb[:, 256:512], w_pool_ref[1])
    return (jnp.concatenate([lo, hi], axis=1) * s_pool_ref[...]).astype(BF16)


def _spatial_gate(vb, w_s_ref, bias_ref):
    causal = (lax.broadcasted_iota(jnp.int32, (CHUNK, CHUNK), 0)
              >= lax.broadcasted_iota(jnp.int32, (CHUNK, CHUNK), 1))
    w_causal = [jnp.where(causal, w_s_ref[g], 0.0).astype(BF16)
                for g in range(N_GMLP_GROUPS)]
    pair_w = 2 * GMLP_GROUP
    w_pairs = [jnp.concatenate(w_causal[2 * p:2 * p + 2], axis=1)
               for p in range(N_GMLP_GROUPS // 2)]
    first = lax.broadcasted_iota(jnp.int32, (CHUNK, pair_w), 1) < GMLP_GROUP
    no_v = jnp.zeros((CHUNK, pair_w), BF16)
    rows = []
    for c in range(vb.shape[0] // CHUNK):
        cols = []
        for p in range(N_GMLP_GROUPS // 2):
            v_pair = vb[c * CHUNK:(c + 1) * CHUNK, p * pair_w:(p + 1) * pair_w]
            v_diag = jnp.concatenate([jnp.where(first, v_pair, no_v),
                                      jnp.where(first, no_v, v_pair)], axis=0)
            cols.append(_dot(w_pairs[p], v_diag))
        rows.append(jnp.concatenate(cols, axis=1) + bias_ref[...])
    return jnp.concatenate(rows, axis=0)


def _mixer_first_step(x_ref, w_in_ref, g_mix_ref, w_pool_ref, s_pool_ref, w_s_ref,
                      bias_ref, g_v_ref, w_po_ref, w_go_ref, w_out_ref, g_ffn_ref,
                      pool_ref, halo_ref, x1buf, h2buf):
    x = x_ref[0]
    h = _rms(x, g_mix_ref[...]).astype(BF16)
    a = _dot(h, w_in_ref[:, 0:O1])
    z_v = _dot(h, w_in_ref[:, O1 + GMLP_WIDTH:O2])
    z_u = _dot(h, w_in_ref[:, O1:O1 + GMLP_WIDTH])
    pb = _pool_tile(a, 0, 0, jnp.zeros((HALO, POOL_WIDTH), F32), halo_ref, pool_ref)
    gate_a = _sigmoid(_dot(h, w_in_ref[:, O2:O3]))
    vb = _rms(jax.nn.gelu(z_v), g_v_ref[...]).astype(BF16)
    gate_b = _sigmoid(_dot(h, w_in_ref[:, O3:IN_WIDTH]))
    pa = _pool_project(pb, w_pool_ref, s_pool_ref)
    sg = (jax.nn.gelu(z_u) * _spatial_gate(vb, w_s_ref, bias_ref)).astype(BF16)
    merged = (gate_a * _dot(pa, w_po_ref[...])
              + gate_b * _dot(sg, w_go_ref[...])).astype(BF16)
    x1 = x + _dot(merged, w_out_ref[...])
    x1buf[...] = x1
    h2buf[...] = _rms(x1, g_ffn_ref[...]).astype(BF16)


def _ffn_last_step(w_gate_ref, w_up_ref, w_down_ref, g_final_ref, y_ref, x1buf, h2buf):
    h2 = h2buf[...]
    act = (_silu(_dot(h2, w_gate_ref[...])) * _dot(h2, w_up_ref[...])).astype(BF16)
    x2 = x1buf[...] + _dot(act, w_down_ref[...])
    y_ref[0] = _rms(x2, g_final_ref[...])


def _steady_step(b, t, x_ref, w_in_ref, g_mix_ref, w_pool_ref, s_pool_ref, w_s_ref,
                 bias_ref, g_v_ref, w_po_ref, w_go_ref, w_out_ref, g_ffn_ref,
                 w_gate_ref, w_up_ref, w_down_ref, g_final_ref,
                 y_ref, pool_ref, halo_ref, x1buf, h2buf):
    tm = x_ref.shape[1]
    half = D_MODEL // 2
    quarter = D_MODEL // 4

    h2 = h2buf[...]
    acts = {}
    ffn_acc = []

    def gate_up(c):
        cols = slice(c * FFN_CHUNK, (c + 1) * FFN_CHUNK)
        acts[c] = (_silu(_dot(h2, w_gate_ref[:, cols]))
                   * _dot(h2, w_up_ref[:, cols])).astype(BF16)

    def down(c):
        part = _dot(acts.pop(c), w_down_ref[c * FFN_CHUNK:(c + 1) * FFN_CHUNK, :])
        ffn_acc.append(part + (ffn_acc[-1] if ffn_acc else x1buf[...]))

    def in_gate(base, i):
        return _sigmoid(_dot(h, w_in_ref[:, base + i * quarter:base + (i + 1) * quarter]))

    x = x_ref[0]
    gate_up(0)
    h = _rms(x, g_mix_ref[...]).astype(BF16)
    gate_up(1)
    a = _dot(h, w_in_ref[:, 0:O1])
    down(0)
    gate_up(2)
    z_v = _dot(h, w_in_ref[:, O1 + GMLP_WIDTH:O2])
    down(1)
    gate_up(3)
    pb = _pool_tile(a, b, t, jnp.where(t == 0, 0.0, halo_ref[...]), halo_ref, pool_ref)
    z_u = _dot(h, w_in_ref[:, O1:O1 + GMLP_WIDTH])
    down(2)
    gate_up(4)
    vb = _rms(jax.nn.gelu(z_v), g_v_ref[...]).astype(BF16)
    pa = _pool_project(pb, w_pool_ref, s_pool_ref)
    gate_a = [in_gate(O2, 0), in_gate(O2, 1)]
    down(3)
    gate_up(5)
    u = jax.nn.gelu(z_u)
    s_gate = _spatial_gate(vb, w_s_ref, bias_ref)
    gate_a += [in_gate(O2, 2), in_gate(O2, 3)]
    down(4)
    gate_up(6)
    sg = (u * s_gate).astype(BF16)
    gate_b = [in_gate(O3, 0), in_gate(O3, 1)]
    down(5)
    gate_up(7)
    gate_b += [in_gate(O3, 2), in_gate(O3, 3)]
    po_lo = _dot(pa, w_po_ref[:, 0:half])
    go_lo = _dot(sg, w_go_ref[:, 0:half])
    down(6)
    gate_up(8)
    merged_lo = (jnp.concatenate(gate_a[0:2], axis=1) * po_lo
                 + jnp.concatenate(gate_b[0:2], axis=1) * go_lo).astype(BF16)
    po_hi = _dot(pa, w_po_ref[:, half:])
    go_hi = _dot(sg, w_go_ref[:, half:])
    down(7)
    gate_up(9)
    merged_hi = (jnp.concatenate(gate_a[2:4], axis=1) * po_hi
                 + jnp.concatenate(gate_b[2:4], axis=1) * go_hi).astype(BF16)
    out_lo = _dot(merged_lo, w_out_ref[0:half, :])
    down(8)
    gate_up(10)
    x1 = x + out_lo + _dot(merged_hi, w_out_ref[half:, :])
    down(9)
    x1buf[...] = x1
    h2buf[...] = _rms(x1, g_ffn_ref[...]).astype(BF16)
    down(10)
    y_ref[0] = _rms(ffn_acc[-1], g_final_ref[...])


def _load_weights(jobs, *scoped):
    widths = sorted({dst.shape[1] for _, dst in jobs})
    stages = dict(zip(widths, scoped[:-1]))
    sems = scoped[-1]
    plan = []
    used = {c: 0 for c in widths}
    for src, dst in jobs:
        n_rows, c = dst.shape
        rows = stages[c].shape[1]
        for r0 in range(0, n_rows, rows):
            plan.append((src, dst, r0, min(rows, n_rows - r0), c, used[c] % STAGE_SLOTS))
            used[c] += 1

    def copy(i):
        src, _, r0, rr, c, slot = plan[i]
        return pltpu.make_async_copy(src.at[0, pl.ds(r0, rr), :],
                                     stages[c].at[slot, pl.ds(0, rr), :],
                                     sems.at[i % STAGE_SLOTS])

    for i in range(min(STAGE_LOOKAHEAD, len(plan))):
        copy(i).start()
    for i, (_, dst, r0, rr, c, slot) in enumerate(plan):
        if i + STAGE_LOOKAHEAD < len(plan):
            copy(i + STAGE_LOOKAHEAD).start()
        copy(i).wait()
        dst[pl.ds(r0, rr), :] = stages[c][slot, 0:rr, :].astype(BF16)


def _stage_rows(width):
    rows = STAGE_BYTES // (4 * width)
    return max(16, rows - rows % 16)


def _sample_rows(xs_ref, state_hbm, pool_s_hbm, w_in_ref, g_mix_ref, w_pool_ref,
                 s_pool_ref, w_s_ref, b_s_ref, g_v_ref, w_po_ref, w_go_ref,
                 w_out_ref, g_ffn_ref, w_gate_ref, w_up_ref, w_down_ref, g_final_ref,
                 ys_ref, vs_ref, before_compute, state_buf, a_buf, sems):
    load_state = pltpu.make_async_copy(state_hbm, state_buf, sems.at[0])
    shift_state = pltpu.make_async_copy(state_buf.at[pl.ds(1, POOL_STATE - 1)],
                                        pool_s_hbm.at[pl.ds(0, POOL_STATE - 1)],
                                        sems.at[1])
    store_row = pltpu.make_async_copy(a_buf, pool_s_hbm.at[POOL_STATE - 1], sems.at[2])
    load_state.start()
    before_compute()

    x = xs_ref[...]
    h = _rms(x, g_mix_ref[...]).astype(BF16)
    a = _dot(h, w_in_ref[:, 0:O1])
    a_buf[...] = a
    store_row.start()
    z = jax.nn.gelu(_dot(h, w_in_ref[:, O1:O2]))
    gate_a = _sigmoid(_dot(h, w_in_ref[:, O2:O3]))
    gate_b = _sigmoid(_dot(h, w_in_ref[:, O3:IN_WIDTH]))

    load_state.wait()
    shift_state.start()
    parts = []
    for gi, w in enumerate(POOL_WINDOWS):
        lanes = slice(gi * POOL_GROUP, (gi + 1) * POOL_GROUP)
        xg = a[:, lanes]
        wsum = xg
        for k in range(1, w):
            wsum = wsum + state_buf[POOL_STATE - k, :, lanes]
        parts.append(wsum * (1.0 / w) - xg)
    pa = _pool_project(jnp.concatenate(parts, axis=1).astype(BF16), w_pool_ref, s_pool_ref)

    u = z[:, 0:GMLP_WIDTH]
    v = _rms(z[:, GMLP_WIDTH:], g_v_ref[...])
    vs_ref[...] = v
    w_diag = jnp.concatenate(
        [jnp.broadcast_to(w_s_ref[g, 0:1, 0:1], (1, GMLP_GROUP))
         for g in range(N_GMLP_GROUPS)], axis=1)
    bias0 = jnp.concatenate(
        [jnp.broadcast_to(b_s_ref[g:g + 1, 0:1], (1, GMLP_GROUP))
         for g in range(N_GMLP_GROUPS)], axis=1)
    sg = (u * (w_diag * v + bias0)).astype(BF16)

    merged = (gate_a * _dot(pa, w_po_ref[...])
              + gate_b * _dot(sg, w_go_ref[...])).astype(BF16)
    x1 = x + _dot(merged, w_out_ref[...])
    h2 = _rms(x1, g_ffn_ref[...]).astype(BF16)
    act = (_silu(_dot(h2, w_gate_ref[...])) * _dot(h2, w_up_ref[...])).astype(BF16)
    x2 = x1 + _dot(act, w_down_ref[...])
    ys_ref[...] = _rms(x2, g_final_ref[...])
    shift_state.wait()
    store_row.wait()


def _prepare_small(w_pool_f32_ref, b_s_ref, w_pool_ref, bias_ref):
    w_pool_ref[...] = jnp.zeros(w_pool_ref.shape, BF16)
    for g in range(len(POOL_WINDOWS)):
        lo = (g % 2) * POOL_GROUP
        w_pool_ref[g // 2, lo:lo + POOL_GROUP, lo:lo + POOL_GROUP] = (
            w_pool_f32_ref[g].astype(BF16))
    for g in range(N_GMLP_GROUPS):
        rows = jnp.broadcast_to(b_s_ref[g:g + 1, :], (GMLP_GROUP, CHUNK))
        bias_ref[:, g * GMLP_GROUP:(g + 1) * GMLP_GROUP] = jnp.transpose(rows)


def _layer_kernel(x_ref, xs_ref, state_hbm, w_in_hbm, w_po_hbm, w_go_hbm, w_out_hbm,
                  w_gate_hbm, w_up_hbm, w_down_hbm, g_mix_ref, w_pool_f32_ref,
                  s_pool_ref, w_s_ref, b_s_ref, g_v_ref, g_ffn_ref, g_final_ref,
                  y_ref, pool_ref, ys_ref, vs_ref, pool_s_hbm,
                  w_in_ref, w_po_ref, w_go_ref, w_out_ref, w_gate_ref, w_up_ref,
                  w_down_ref, w_pool_ref, bias_ref, halo_ref, x1buf, h2buf,
                  *, n_tiles, tiles_per_seq):
    s = pl.program_id(0)
    mixer_refs = (x_ref, w_in_ref, g_mix_ref, w_pool_ref, s_pool_ref, w_s_ref, bias_ref,
                  g_v_ref, w_po_ref, w_go_ref, w_out_ref, g_ffn_ref)
    ffn_refs = (w_gate_ref, w_up_ref, w_down_ref, g_final_ref)
    carry = (halo_ref, x1buf, h2buf)

    @pl.when(s == 0)
    def _():
        _prepare_small(w_pool_f32_ref, b_s_ref, w_pool_ref, bias_ref)
        jobs = ((w_in_hbm, w_in_ref), (w_po_hbm, w_po_ref), (w_go_hbm, w_go_ref),
                (w_out_hbm, w_out_ref), (w_gate_hbm, w_gate_ref), (w_up_hbm, w_up_ref),
                (w_down_hbm, w_down_ref))
        widths = sorted({dst.shape[1] for _, dst in jobs})
        pl.run_scoped(
            functools.partial(_load_weights, jobs),
            *[pltpu.VMEM((STAGE_SLOTS, _stage_rows(c), c), F32) for c in widths],
            pltpu.SemaphoreType.DMA((STAGE_SLOTS,)))
        pl.run_scoped(
            functools.partial(
                _sample_rows, xs_ref, state_hbm, pool_s_hbm, w_in_ref, g_mix_ref,
                w_pool_ref, s_pool_ref, w_s_ref, b_s_ref, g_v_ref, w_po_ref,
                w_go_ref, w_out_ref, g_ffn_ref, *ffn_refs, ys_ref, vs_ref,
                functools.partial(_mixer_first_step, *mixer_refs, pool_ref, *carry)),
            pltpu.VMEM(state_hbm.shape, F32),
            pltpu.VMEM((state_hbm.shape[1], POOL_WIDTH), F32),
            pltpu.SemaphoreType.DMA((3,)))

    @pl.when(jnp.logical_and(s > 0, s < n_tiles))
    def _():
        _steady_step(lax.div(s, tiles_per_seq), lax.rem(s, tiles_per_seq),
                     *mixer_refs, *ffn_refs, y_ref, pool_ref, *carry)

    @pl.when(s == n_tiles)
    def _():
        _ffn_last_step(*ffn_refs, y_ref, x1buf, h2buf)


def _resident(shape):
    zeros = (0,) * len(shape)
    return pl.BlockSpec(shape, lambda s: zeros, pipeline_mode=pl.Buffered(1))


def kernel(x_prompt, x_sample, state_pool, w_in, g_mix, w_pool, s_pool, w_s, b_s, g_v,
           w_pool_out, w_gmlp_out, w_out, g_ffn, w_gate, w_up, w_down, g_final):
    batch, seq, _ = x_prompt.shape
    dec_batch = x_sample.shape[0]
    assert w_in.shape[0] == 1 and x_sample.shape[1] == 1
    assert seq % ROW_TILE == 0 and ROW_TILE % CHUNK == 0
    assert N_FFN_CHUNKS == 11

    g_mix2 = g_mix[0][None, :]
    s_pool2 = s_pool[0][None, :]
    g_v2 = g_v[0][None, :]
    g_ffn2 = g_ffn[0][None, :]
    g_final2 = g_final[None, :]
    state_t = jnp.transpose(state_pool[0], (1, 0, 2))

    tiles_per_seq = seq // ROW_TILE
    n_tiles = batch * tiles_per_seq

    def mixer_tile(s):
        i = jnp.minimum(s, n_tiles - 1)
        return i // tiles_per_seq, i % tiles_per_seq

    def ffn_tile(s):
        i = jnp.maximum(s - 1, 0)
        return i // tiles_per_seq, i % tiles_per_seq

    big_weights = (w_in, w_pool_out, w_gmlp_out, w_out, w_gate, w_up, w_down)
    small = (g_mix2, w_pool[0], s_pool2, w_s[0], b_s[0], g_v2, g_ffn2, g_final2)
    x_s = x_sample[:, 0, :]
    in_specs = [pl.BlockSpec((1, ROW_TILE, D_MODEL), lambda s: (*mixer_tile(s), 0)),
                _resident(x_s.shape)]
    in_specs += [pl.BlockSpec(memory_space=pl.ANY)] * (1 + len(big_weights))
    in_specs += [_resident(arr.shape) for arr in small]
    y_prompt, pool_p, y_s, v_s, pool_s = pl.pallas_call(
        functools.partial(_layer_kernel, n_tiles=n_tiles, tiles_per_seq=tiles_per_seq),
        grid=(n_tiles + 1,),
        in_specs=in_specs,
        out_specs=[
            pl.BlockSpec((1, ROW_TILE, D_MODEL), lambda s: (*ffn_tile(s), 0)),
            pl.BlockSpec((POOL_STATE, batch, POOL_WIDTH), lambda s: (0, 0, 0)),
            pl.BlockSpec((dec_batch, D_MODEL), lambda s: (0, 0)),
            pl.BlockSpec((dec_batch, GMLP_WIDTH), lambda s: (0, 0)),
            pl.BlockSpec(memory_space=pl.ANY),
        ],
        out_shape=[
            jax.ShapeDtypeStruct((batch, seq, D_MODEL), F32),
            jax.ShapeDtypeStruct((POOL_STATE, batch, POOL_WIDTH), F32),
            jax.ShapeDtypeStruct((dec_batch, D_MODEL), F32),
            jax.ShapeDtypeStruct((dec_batch, GMLP_WIDTH), F32),
            jax.ShapeDtypeStruct((POOL_STATE, dec_batch, POOL_WIDTH), F32),
        ],
        scratch_shapes=[pltpu.VMEM(w.shape[1:], BF16) for w in big_weights] + [
            pltpu.VMEM((2, 2 * POOL_GROUP, 2 * POOL_GROUP), BF16),
            pltpu.VMEM((CHUNK, GMLP_WIDTH), F32),
            pltpu.VMEM((HALO, POOL_WIDTH), F32),
            pltpu.VMEM((ROW_TILE, D_MODEL), F32),
            pltpu.VMEM((ROW_TILE, D_MODEL), BF16),
        ],
        compiler_params=pltpu.CompilerParams(
            dimension_semantics=("arbitrary",),
            vmem_limit_bytes=VMEM_LIMIT_BYTES),
        name="layer",
    )(x_prompt, x_s, state_t, *big_weights, *small)

    return (
        y_prompt,
        y_s[:, None, :],
        jnp.transpose(pool_p, (1, 0, 2))[None],
        jnp.transpose(pool_s, (1, 0, 2))[None],
        v_s[None, :, None, :],
    )
```

```python
import functools

import jax
import jax.numpy as jnp
from jax import lax
from jax.experimental import pallas as pl
from jax.experimental.pallas import tpu as pltpu

D_MODEL = 1024
POOL_WINDOWS = (2, 4, 8, 16)
POOL_WIDTH = 512
POOL_GROUP = 128
POOL_STATE = 15
GMLP_WIDTH = 512
N_GMLP_GROUPS = 4
GMLP_GROUP = 128
CHUNK = 128
D_FF = 2816
EPS = 1e-6

O1 = POOL_WIDTH
O2 = O1 + 2 * GMLP_WIDTH
O3 = O2 + D_MODEL
IN_WIDTH = O3 + D_MODEL

HALO = 16
ROW_TILE = 256
FFN_CHUNK = 256
N_FFN_CHUNKS = D_FF // FFN_CHUNK
VMEM_LIMIT_BYTES = 60 * 1024 * 1024

STAGE_SLOTS = 4
STAGE_LOOKAHEAD = STAGE_SLOTS - 1
STAGE_BYTES = 1 << 20

F32 = jnp.float32
BF16 = jnp.bfloat16


def _dot(a, b):
    return jnp.dot(a, b, preferred_element_type=F32)


def _rms(x, g):
    ms = jnp.mean(x * x, axis=-1, keepdims=True)
    return x * lax.rsqrt(ms + EPS) * g


def _sigmoid(x):
    return 0.5 * jnp.tanh(0.5 * x) + 0.5


def _silu(x):
    h = 0.5 * x
    return h * jnp.tanh(h) + h


def _window_sums(ext, w):
    s = ext
    k = 1
    while k < w:
        s = s + pltpu.roll(s, k, axis=0)
        k *= 2
    return s


def _pool_tile(a, b, t, halo, halo_ref, pool_ref):
    tm = a.shape[0]
    ext = jnp.concatenate([halo, a], axis=0)
    pos = lax.broadcasted_iota(jnp.int32, (tm, 1), 0) + t * tm
    parts = []
    for gi, w in enumerate(POOL_WINDOWS):
        lanes = slice(gi * POOL_GROUP, (gi + 1) * POOL_GROUP)
        wsum = _window_sums(ext[:, lanes], w)[HALO:]
        inv_cnt = 1.0 / jnp.minimum(w, pos + 1).astype(F32)
        parts.append(wsum * inv_cnt - a[:, lanes])
    tail = a[tm - HALO:, :]
    for j in range(POOL_STATE):
        row = HALO - POOL_STATE + j
        pool_ref[j, pl.ds(b, 1), :] = tail[row:row + 1, :]
    halo_ref[...] = tail
    return jnp.concatenate(parts, axis=1).astype(BF16)


def _pool_project(pb, w_pool_ref, s_pool_ref):
    lo = _dot(pb[:, 0:256], w_pool_ref[0])
    hi = _dot(pb[:, 256:512], w_pool_ref[1])
    return (jnp.concatenate([lo, hi], axis=1) * s_pool_ref[...]).astype(BF16)


def _spatial_gate(vb, w_s_ref, bias_ref):
    causal = (lax.broadcasted_iota(jnp.int32, (CHUNK, CHUNK), 0)
              >= lax.broadcasted_iota(jnp.int32, (CHUNK, CHUNK), 1))
    w_causal = [jnp.where(causal, w_s_ref[g], 0.0).astype(BF16)
                for g in range(N_GMLP_GROUPS)]
    pair_w = 2 * GMLP_GROUP
    w_pairs = [jnp.concatenate(w_causal[2 * p:2 * p + 2], axis=1)
               for p in range(N_GMLP_GROUPS // 2)]
    first = lax.broadcasted_iota(jnp.int32, (CHUNK, pair_w), 1) < GMLP_GROUP
    no_v = jnp.zeros((CHUNK, pair_w), BF16)
    rows = []
    for c in range(vb.shape[0] // CHUNK):
        cols = []
        for p in range(N_GMLP_GROUPS // 2):
            v_pair = vb[c * CHUNK:(c + 1) * CHUNK, p * pair_w:(p + 1) * pair_w]
            v_diag = jnp.concatenate([jnp.where(first, v_pair, no_v),
                                      jnp.where(first, no_v, v_pair)], axis=0)
            cols.append(_dot(w_pairs[p], v_diag))
        rows.append(jnp.concatenate(cols, axis=1) + bias_ref[...])
    return jnp.concatenate(rows, axis=0)


def _gate_up_chunk(h2, c, w_gate_ref, w_up_ref):
    cols = slice(c * FFN_CHUNK, (c + 1) * FFN_CHUNK)
    return (_silu(_dot(h2, w_gate_ref[:, cols])) * _dot(h2, w_up_ref[:, cols])).astype(BF16)


def _mixer_first_step(x_ref, w_in_ref, g_mix_ref, w_pool_ref, s_pool_ref, w_s_ref,
                      bias_ref, g_v_ref, w_po_ref, w_go_ref, w_out_ref, g_ffn_ref,
                      w_gate_ref, w_up_ref, pool_ref, halo_ref, x1buf, h2buf, act0buf):
    x = x_ref[0]
    h = _rms(x, g_mix_ref[...]).astype(BF16)
    a = _dot(h, w_in_ref[:, 0:O1])
    z_v = _dot(h, w_in_ref[:, O1 + GMLP_WIDTH:O2])
    z_u = _dot(h, w_in_ref[:, O1:O1 + GMLP_WIDTH])
    pb = _pool_tile(a, 0, 0, jnp.zeros((HALO, POOL_WIDTH), F32), halo_ref, pool_ref)
    gate_a = _sigmoid(_dot(h, w_in_ref[:, O2:O3]))
    vb = _rms(jax.nn.gelu(z_v), g_v_ref[...]).astype(BF16)
    gate_b = _sigmoid(_dot(h, w_in_ref[:, O3:IN_WIDTH]))
    pa = _pool_project(pb, w_pool_ref, s_pool_ref)
    sg = (jax.nn.gelu(z_u) * _spatial_gate(vb, w_s_ref, bias_ref)).astype(BF16)
    merged = (gate_a * _dot(pa, w_po_ref[...])
              + gate_b * _dot(sg, w_go_ref[...])).astype(BF16)
    x1 = x + _dot(merged, w_out_ref[...])
    x1buf[...] = x1
    h2 = _rms(x1, g_ffn_ref[...]).astype(BF16)
    h2buf[...] = h2
    act0buf[...] = _gate_up_chunk(h2, 0, w_gate_ref, w_up_ref)


def _ffn_last_step(w_gate_ref, w_up_ref, w_down_ref, g_final_ref, y_ref, x1buf, h2buf):
    h2 = h2buf[...]
    act = (_silu(_dot(h2, w_gate_ref[...])) * _dot(h2, w_up_ref[...])).astype(BF16)
    x2 = x1buf[...] + _dot(act, w_down_ref[...])
    y_ref[0] = _rms(x2, g_final_ref[...])


def _steady_step(b, t, x_ref, w_in_ref, g_mix_ref, w_pool_ref, s_pool_ref, w_s_ref,
                 bias_ref, g_v_ref, w_po_ref, w_go_ref, w_out_ref, g_ffn_ref,
                 w_gate_ref, w_up_ref, w_down_ref, g_final_ref,
                 y_ref, pool_ref, halo_ref, x1buf, h2buf, act0buf):
    half = D_MODEL // 2
    quarter = D_MODEL // 4

    h2 = h2buf[...]
    acts = {0: act0buf[...]}
    ffn_acc = []

    def gate_up(c):
        acts[c] = _gate_up_chunk(h2, c, w_gate_ref, w_up_ref)

    def down(c):
        part = _dot(acts.pop(c), w_down_ref[c * FFN_CHUNK:(c + 1) * FFN_CHUNK, :])
        ffn_acc.append(part + (ffn_acc[-1] if ffn_acc else x1buf[...]))

    def in_gate(base, i):
        return _sigmoid(_dot(h, w_in_ref[:, base + i * quarter:base + (i + 1) * quarter]))

    x = x_ref[0]
    gate_up(1)
    h = _rms(x, g_mix_ref[...]).astype(BF16)
    gate_up(2)
    a = _dot(h, w_in_ref[:, 0:O1])
    down(0)
    gate_up(3)
    z_v = _dot(h, w_in_ref[:, O1 + GMLP_WIDTH:O2])
    down(1)
    gate_up(4)
    pb = _pool_tile(a, b, t, jnp.where(t == 0, 0.0, halo_ref[...]), halo_ref, pool_ref)
    z_u = _dot(h, w_in_ref[:, O1:O1 + GMLP_WIDTH])
    down(2)
    gate_up(5)
    vb = _rms(jax.nn.gelu(z_v), g_v_ref[...]).astype(BF16)
    pa = _pool_project(pb, w_pool_ref, s_pool_ref)
    gate_a = [in_gate(O2, 0), in_gate(O2, 1)]
    down(3)
    gate_up(6)
    u = jax.nn.gelu(z_u)
    s_gate = _spatial_gate(vb, w_s_ref, bias_ref)
    gate_a += [in_gate(O2, 2), in_gate(O2, 3)]
    down(4)
    gate_up(7)
    sg = (u * s_gate).astype(BF16)
    gate_b = [in_gate(O3, 0), in_gate(O3, 1)]
    down(5)
    gate_up(8)
    gate_b += [in_gate(O3, 2), in_gate(O3, 3)]
    po_lo = _dot(pa, w_po_ref[:, 0:half])
    go_lo = _dot(sg, w_go_ref[:, 0:half])
    down(6)
    gate_up(9)
    merged_lo =(jnp.concatenate(gate_a[0:2], axis=1) * po_lo
                 + jnp.concatenate(gate_b[0:2], axis=1) * go_lo).astype(BF16)
    po_hi = _dot(pa, w_po_ref[:, half:])
    go_hi = _dot(sg, w_go_ref[:, half:])
    down(7)
    gate_up(10)
    merged_hi = (jnp.concatenate(gate_a[2:4], axis=1) * po_hi
                 + jnp.concatenate(gate_b[2:4], axis=1) * go_hi).astype(BF16)
    out_lo = _dot(merged_lo, w_out_ref[0:half, :])
    down(8)
    x1 = x + out_lo + _dot(merged_hi, w_out_ref[half:, :])
    down(9)
    x1buf[...] = x1
    h2_next = _rms(x1, g_ffn_ref[...]).astype(BF16)
    h2buf[...] = h2_next
    down(10)
    act0_next = _gate_up_chunk(h2_next, 0, w_gate_ref, w_up_ref)
    y_ref[0] = _rms(ffn_acc[-1], g_final_ref[...])
    act0buf[...] = act0_next


def _load_weights(jobs, *scoped):
    widths = sorted({dst.shape[1] for _, dst in jobs})
    stages = dict(zip(widths, scoped[:-1]))
    sems = scoped[-1]
    plan = []
    used = {c: 0 for c in widths}
    for src, dst in jobs:
        n_rows, c = dst.shape
        rows = stages[c].shape[1]
        for r0 in range(0, n_rows, rows):
            plan.append((src, dst, r0, min(rows, n_rows - r0), c, used[c] % STAGE_SLOTS))
            used[c] += 1

    def copy(i):
        src, _, r0, rr, c, slot = plan[i]
        return pltpu.make_async_copy(src.at[0, pl.ds(r0, rr), :],
                                     stages[c].at[slot, pl.ds(0, rr), :],
                                     sems.at[i % STAGE_SLOTS])

    for i in range(min(STAGE_LOOKAHEAD, len(plan))):
        copy(i).start()
    for i, (_, dst, r0, rr, c, slot) in enumerate(plan):
        if i + STAGE_LOOKAHEAD < len(plan):
            copy(i + STAGE_LOOKAHEAD).start()
        copy(i).wait()
        dst[pl.ds(r0, rr), :] = stages[c][slot, 0:rr, :].astype(BF16)


def _stage_rows(width):
    rows = STAGE_BYTES // (4 * width)
    return max(16, rows - rows % 16)


def _sample_rows(xs_ref, state_hbm, pool_s_hbm, w_in_ref, g_mix_ref, w_pool_ref,
                 s_pool_ref, w_s_ref, b_s_ref, g_v_ref, w_po_ref, w_go_ref,
                 w_out_ref, g_ffn_ref, w_gate_ref, w_up_ref, w_down_ref, g_final_ref,
                 ys_ref, vs_ref, before_compute, state_buf, a_buf, sems):
    load_state = pltpu.make_async_copy(state_hbm, state_buf, sems.at[0])
    shift_state = pltpu.make_async_copy(state_buf.at[pl.ds(1, POOL_STATE - 1)],
                                        pool_s_hbm.at[pl.ds(0, POOL_STATE - 1)],
                                        sems.at[1])
    store_row = pltpu.make_async_copy(a_buf, pool_s_hbm.at[POOL_STATE - 1], sems.at[2])
    load_state.start()
    before_compute()

    x = xs_ref[...]
    h = _rms(x, g_mix_ref[...]).astype(BF16)
    a = _dot(h, w_in_ref[:, 0:O1])
    a_buf[...] = a
    store_row.start()
    z = jax.nn.gelu(_dot(h, w_in_ref[:, O1:O2]))
    gate_a = _sigmoid(_dot(h, w_in_ref[:, O2:O3]))
    gate_b = _sigmoid(_dot(h, w_in_ref[:, O3:IN_WIDTH]))

    load_state.wait()
    shift_state.start()
    parts = []
    for gi, w in enumerate(POOL_WINDOWS):
        lanes = slice(gi * POOL_GROUP, (gi + 1) * POOL_GROUP)
        xg = a[:, lanes]
        wsum = xg
        for k in range(1, w):
            wsum = wsum + state_buf[POOL_STATE - k, :, lanes]
        parts.append(wsum * (1.0 / w) - xg)
    pa = _pool_project(jnp.concatenate(parts, axis=1).astype(BF16), w_pool_ref, s_pool_ref)

    u = z[:, 0:GMLP_WIDTH]
    v = _rms(z[:, GMLP_WIDTH:], g_v_ref[...])
    vs_ref[...] = v
    w_diag = jnp.concatenate(
        [jnp.broadcast_to(w_s_ref[g, 0:1, 0:1], (1, GMLP_GROUP))
         for g in range(N_GMLP_GROUPS)], axis=1)
    bias0 = jnp.concatenate(
        [jnp.broadcast_to(b_s_ref[g:g + 1, 0:1], (1, GMLP_GROUP))
         for g in range(N_GMLP_GROUPS)], axis=1)
    sg = (u * (w_diag * v + bias0)).astype(BF16)

    merged = (gate_a * _dot(pa, w_po_ref[...])
              + gate_b * _dot(sg, w_go_ref[...])).astype(BF16)
    x1 = x + _dot(merged, w_out_ref[...])
    h2 = _rms(x1, g_ffn_ref[...]).astype(BF16)
    act = (_silu(_dot(h2, w_gate_ref[...])) * _dot(h2, w_up_ref[...])).astype(BF16)
    x2 = x1 + _dot(act, w_down_ref[...])
    ys_ref[...] = _rms(x2, g_final_ref[...])
    shift_state.wait()
    store_row.wait()


def _prepare_small(w_pool_f32_ref, b_s_ref, w_pool_ref, bias_ref):
    w_pool_ref[...] = jnp.zeros(w_pool_ref.shape, BF16)
    for g in range(len(POOL_WINDOWS)):
        lo = (g % 2) * POOL_GROUP
        w_pool_ref[g // 2, lo:lo + POOL_GROUP, lo:lo + POOL_GROUP] = (
            w_pool_f32_ref[g].astype(BF16))
    for g in range(N_GMLP_GROUPS):
        rows = jnp.broadcast_to(b_s_ref[g:g + 1, :], (GMLP_GROUP, CHUNK))
        bias_ref[:, g * GMLP_GROUP:(g + 1) * GMLP_GROUP] = jnp.transpose(rows)


def _layer_kernel(x_ref, xs_ref, state_hbm, w_in_hbm, w_po_hbm, w_go_hbm, w_out_hbm,
                  w_gate_hbm, w_up_hbm, w_down_hbm, g_mix_ref, w_pool_f32_ref,
                  s_pool_ref, w_s_ref, b_s_ref, g_v_ref, g_ffn_ref, g_final_ref,
                  y_ref, pool_ref, ys_ref, vs_ref, pool_s_hbm,
                  w_in_ref, w_po_ref, w_go_ref, w_out_ref, w_gate_ref, w_up_ref,
                  w_down_ref, w_pool_ref, bias_ref, halo_ref, x1buf, h2buf, act0buf,
                  *, n_tiles, tiles_per_seq):
    s = pl.program_id(0)
    mixer_refs = (x_ref, w_in_ref, g_mix_ref, w_pool_ref, s_pool_ref, w_s_ref, bias_ref,
                  g_v_ref, w_po_ref, w_go_ref, w_out_ref, g_ffn_ref)
    ffn_refs = (w_gate_ref, w_up_ref, w_down_ref, g_final_ref)
    carry = (halo_ref, x1buf, h2buf, act0buf)

    @pl.when(s == 0)
    def _():
        _prepare_small(w_pool_f32_ref, b_s_ref, w_pool_ref, bias_ref)
        jobs = ((w_in_hbm, w_in_ref), (w_po_hbm, w_po_ref), (w_go_hbm, w_go_ref),
                (w_out_hbm, w_out_ref), (w_gate_hbm, w_gate_ref), (w_up_hbm, w_up_ref),
                (w_down_hbm, w_down_ref))
        widths = sorted({dst.shape[1] for _, dst in jobs})
        pl.run_scoped(
            functools.partial(_load_weights, jobs),
            *[pltpu.VMEM((STAGE_SLOTS, _stage_rows(c), c), F32) for c in widths],
            pltpu.SemaphoreType.DMA((STAGE_SLOTS,)))
        pl.run_scoped(
            functools.partial(
                _sample_rows, xs_ref, state_hbm, pool_s_hbm, w_in_ref, g_mix_ref,
                w_pool_ref, s_pool_ref, w_s_ref, b_s_ref, g_v_ref, w_po_ref,
                w_go_ref, w_out_ref, g_ffn_ref, *ffn_refs, ys_ref, vs_ref,
                functools.partial(_mixer_first_step, *mixer_refs, w_gate_ref, w_up_ref,
                                  pool_ref, *carry)),
            pltpu.VMEM(state_hbm.shape, F32),
            pltpu.VMEM((state_hbm.shape[1], POOL_WIDTH), F32),
            pltpu.SemaphoreType.DMA((3,)))

    @pl.when(jnp.logical_and(s > 0, s < n_tiles))
    def _():
        _steady_step(lax.div(s, tiles_per_seq), lax.rem(s, tiles_per_seq),
                     *mixer_refs, *ffn_refs, y_ref, pool_ref, *carry)

    @pl.when(s == n_tiles)
    def _():
        _ffn_last_step(*ffn_refs, y_ref, x1buf, h2buf)


def _resident(shape):
    zeros = (0,) * len(shape)
    return pl.BlockSpec(shape, lambda s: zeros, pipeline_mode=pl.Buffered(1))


def kernel(x_prompt, x_sample, state_pool, w_in, g_mix, w_pool, s_pool, w_s, b_s, g_v,
           w_pool_out, w_gmlp_out, w_out, g_ffn, w_gate, w_up, w_down, g_final):
    batch, seq, _ = x_prompt.shape
    dec_batch = x_sample.shape[0]
    assert w_in.shape[0] == 1 and x_sample.shape[1] == 1
    assert seq % ROW_TILE == 0 and ROW_TILE % CHUNK == 0
    assert N_FFN_CHUNKS == 11

    g_mix2 = g_mix[0][None, :]
    s_pool2 = s_pool[0][None, :]
    g_v2 = g_v[0][None, :]
    g_ffn2 = g_ffn[0][None, :]
    g_final2 = g_final[None, :]
    state_t = jnp.transpose(state_pool[0], (1, 0, 2))

    tiles_per_seq = seq // ROW_TILE
    n_tiles = batch * tiles_per_seq

    def mixer_tile(s):
        i = jnp.minimum(s, n_tiles - 1)
        return i // tiles_per_seq, i % tiles_per_seq

    def ffn_tile(s):
        i = jnp.maximum(s - 1, 0)
        return i // tiles_per_seq, i % tiles_per_seq

    big_weights = (w_in, w_pool_out, w_gmlp_out, w_out, w_gate, w_up, w_down)
    small = (g_mix2, w_pool[0], s_pool2, w_s[0], b_s[0], g_v2, g_ffn2, g_final2)
    x_s = x_sample[:, 0, :]
    in_specs = [pl.BlockSpec((1, ROW_TILE, D_MODEL), lambda s: (*mixer_tile(s), 0)),
                _resident(x_s.shape)]
    in_specs += [pl.BlockSpec(memory_space=pl.ANY)] * (1 + len(big_weights))
    in_specs += [_resident(arr.shape) for arr in small]
    y_prompt, pool_p, y_s, v_s, pool_s = pl.pallas_call(
        functools.partial(_layer_kernel, n_tiles=n_tiles, tiles_per_seq=tiles_per_seq),
        grid=(n_tiles + 1,),
        in_specs=in_specs,
        out_specs=[
            pl.BlockSpec((1, ROW_TILE, D_MODEL), lambda s: (*ffn_tile(s), 0)),
            pl.BlockSpec((POOL_STATE, batch, POOL_WIDTH), lambda s: (0, 0, 0)),
            pl.BlockSpec((dec_batch, D_MODEL), lambda s: (0, 0)),
            pl.BlockSpec((dec_batch, GMLP_WIDTH), lambda s: (0, 0)),
            pl.BlockSpec(memory_space=pl.ANY),
        ],
        out_shape=[
            jax.ShapeDtypeStruct((batch, seq, D_MODEL), F32),
            jax.ShapeDtypeStruct((POOL_STATE, batch, POOL_WIDTH), F32),
            jax.ShapeDtypeStruct((dec_batch, D_MODEL), F32),
            jax.ShapeDtypeStruct((dec_batch, GMLP_WIDTH), F32),
            jax.ShapeDtypeStruct((POOL_STATE, dec_batch, POOL_WIDTH), F32),
        ],
        scratch_shapes=[pltpu.VMEM(w.shape[1:], BF16) for w in big_weights] + [
            pltpu.VMEM((2, 2 * POOL_GROUP, 2 * POOL_GROUP), BF16),
            pltpu.VMEM((CHUNK, GMLP_WIDTH), F32),
            pltpu.VMEM((HALO, POOL_WIDTH), F32),
            pltpu.VMEM((ROW_TILE, D_MODEL), F32),
            pltpu.VMEM((ROW_TILE, D_MODEL), BF16),
            pltpu.VMEM((ROW_TILE, FFN_CHUNK), BF16),
        ],
        compiler_params=pltpu.CompilerParams(
            dimension_semantics=("arbitrary",),
            vmem_limit_bytes=VMEM_LIMIT_BYTES),
        name="layer",
    )(x_prompt, x_s, state_t, *big_weights, *small)

    return (
        y_prompt,
        y_s[:, None, :],
        jnp.transpose(pool_p, (1, 0, 2))[None],
        jnp.transpose(pool_s, (1, 0, 2))[None],
        v_s[None, :, None, :],
    )
```

```python
import functools

import jax
import jax.numpy as jnp
from jax import lax
from jax.experimental import pallas as pl
from jax.experimental.pallas import tpu as pltpu

D_MODEL = 1024
POOL_WINDOWS = (2, 4, 8, 16)
POOL_WIDTH = 512
POOL_GROUP = 128
POOL_STATE = 15
GMLP_WIDTH = 512
N_GMLP_GROUPS = 4
GMLP_GROUP = 128
CHUNK = 128
D_FF = 2816
EPS = 1e-6

O1 = POOL_WIDTH
O2 = O1 + 2 * GMLP_WIDTH
O3 = O2 + D_MODEL
IN_WIDTH = O3 + D_MODEL

HALO = 16
ROW_TILE = 256
FFN_CHUNK = 256
N_FFN_CHUNKS = D_FF // FFN_CHUNK
VMEM_LIMIT_BYTES = 60 * 1024 * 1024

STAGE_SLOTS = 5
STAGE_LOOKAHEAD = STAGE_SLOTS - 1
STAGE_BYTES = 1 << 20
FIRST_STEP_TICK_CHUNKS = 3

F32 = jnp.float32
BF16 = jnp.bfloat16


def _dot(a, b):
    return jnp.dot(a, b, preferred_element_type=F32)


def _rms(x, g):
    ms = jnp.mean(x * x, axis=-1, keepdims=True)
    return x * lax.rsqrt(ms + EPS) * g


def _sigmoid(x):
    return 0.5 * jnp.tanh(0.5 * x) + 0.5


def _silu(x):
    h = 0.5 * x
    return h * jnp.tanh(h) + h


def _window_sums(ext, w):
    s = ext
    k = 1
    while k < w:
        s = s + pltpu.roll(s, k, axis=0)
        k *= 2
    return s


def _pool_tile(a, b, t, halo, halo_ref, pool_ref):
    tm = a.shape[0]
    ext = jnp.concatenate([halo, a], axis=0)
    pos = lax.broadcasted_iota(jnp.int32, (tm, 1), 0) + t * tm
    parts = []
    for gi, w in enumerate(POOL_WINDOWS):
        lanes = slice(gi * POOL_GROUP, (gi + 1) * POOL_GROUP)
        wsum = _window_sums(ext[:, lanes], w)[HALO:]
        inv_cnt = 1.0 / jnp.minimum(w, pos + 1).astype(F32)
        parts.append(wsum * inv_cnt - a[:, lanes])
    tail = a[tm - HALO:, :]
    for j in range(POOL_STATE):
        row = HALO - POOL_STATE + j
        pool_ref[j, pl.ds(b, 1), :] = tail[row:row + 1, :]
    halo_ref[...] = tail
    return jnp.concatenate(parts, axis=1).astype(BF16)


def _pool_project(pb, w_pool_ref, s_pool_ref):
    lo = _dot(pb[:, 0:256], w_pool_ref[0])
    hi = _dot(pb[:, 256:512], w_pool_ref[1])
    return (jnp.concatenate([lo, hi], axis=1) * s_pool_ref[...]).astype(BF16)


def _spatial_gate(vb, w_s_ref, bias_ref):
    causal = (lax.broadcasted_iota(jnp.int32, (CHUNK, CHUNK), 0)
              >= lax.broadcasted_iota(jnp.int32, (CHUNK, CHUNK), 1))
    w_causal = [jnp.where(causal, w_s_ref[g], 0.0).astype(BF16)
                for g in range(N_GMLP_GROUPS)]
    pair_w = 2 * GMLP_GROUP
    w_pairs = [jnp.concatenate(w_causal[2 * p:2 * p + 2], axis=1)
               for p in range(N_GMLP_GROUPS // 2)]
    first = lax.broadcasted_iota(jnp.int32, (CHUNK, pair_w), 1) < GMLP_GROUP
    no_v = jnp.zeros((CHUNK, pair_w), BF16)
    rows = []
    for c in range(vb.shape[0] // CHUNK):
        cols = []
        for p in range(N_GMLP_GROUPS // 2):
            v_pair = vb[c * CHUNK:(c + 1) * CHUNK, p * pair_w:(p + 1) * pair_w]
            v_diag = jnp.concatenate([jnp.where(first, v_pair, no_v),
                                      jnp.where(first, no_v, v_pair)], axis=0)
            cols.append(_dot(w_pairs[p], v_diag))
        rows.append(jnp.concatenate(cols, axis=1) + bias_ref[...])
    return jnp.concatenate(rows, axis=0)


def _gate_up_chunk(h2, c, w_gate_ref, w_up_ref):
    cols = slice(c * FFN_CHUNK, (c + 1) * FFN_CHUNK)
    return (_silu(_dot(h2, w_gate_ref[:, cols])) * _dot(h2, w_up_ref[:, cols])).astype(BF16)


def _mixer_first_step(x_ref, w_in_ref, g_mix_ref, w_pool_ref, s_pool_ref, w_s_ref,
                      bias_ref, g_v_ref, w_po_ref, w_go_ref, w_out_ref, g_ffn_ref,
                      pool_ref, halo_ref, x1buf, h2buf, tick):
    x = x_ref[0]
    h = _rms(x, g_mix_ref[...]).astype(BF16)
    a = _dot(h, w_in_ref[:, 0:O1])
    tick()
    z_v = _dot(h, w_in_ref[:, O1 + GMLP_WIDTH:O2])
    tick()
    z_u = _dot(h, w_in_ref[:, O1:O1 + GMLP_WIDTH])
    tick()
    pb = _pool_tile(a, 0, 0, jnp.zeros((HALO, POOL_WIDTH), F32), halo_ref, pool_ref)
    gate_a = [_sigmoid(_dot(h, w_in_ref[:, O2 + i * 512:O2 + (i + 1) * 512]))
              for i in range(2)]
    tick()
    vb = _rms(jax.nn.gelu(z_v), g_v_ref[...]).astype(BF16)
    gate_b = [_sigmoid(_dot(h, w_in_ref[:, O3 + i * 512:O3 + (i + 1) * 512]))
              for i in range(2)]
    tick()
    pa = _pool_project(pb, w_pool_ref, s_pool_ref)
    sg = (jax.nn.gelu(z_u) * _spatial_gate(vb, w_s_ref, bias_ref)).astype(BF16)
    tick()
    merged = (jnp.concatenate(gate_a, axis=1) * _dot(pa, w_po_ref[...])
              + jnp.concatenate(gate_b, axis=1) * _dot(sg, w_go_ref[...])).astype(BF16)
    tick()
    x1 = x + _dot(merged, w_out_ref[...])
    tick()
    x1buf[...] = x1
    h2buf[...] = _rms(x1, g_ffn_ref[...]).astype(BF16)


def _ffn_last_step(w_gate_ref, w_up_ref, w_down_ref, g_final_ref, y_ref, x1buf, h2buf):
    h2 = h2buf[...]
    act = (_silu(_dot(h2, w_gate_ref[...])) * _dot(h2, w_up_ref[...])).astype(BF16)
    x2 = x1buf[...] + _dot(act, w_down_ref[...])
    y_ref[0] = _rms(x2, g_final_ref[...])


def _steady_step(b, t, x_ref, w_in_ref, g_mix_ref, w_pool_ref, s_pool_ref, w_s_ref,
                 bias_ref, g_v_ref, w_po_ref, w_go_ref, w_out_ref, g_ffn_ref,
                 w_gate_ref, w_up_ref, w_down_ref, g_final_ref,
                 y_ref, pool_ref, halo_ref, x1buf, h2buf, act0buf):
    half = D_MODEL // 2
    quarter = D_MODEL // 4

    h2 = h2buf[...]
    acts = {0: act0buf[...]}
    ffn_acc = []

    def gate_up(c):
        acts[c] = _gate_up_chunk(h2, c, w_gate_ref, w_up_ref)

    def down(c):
        part = _dot(acts.pop(c), w_down_ref[c * FFN_CHUNK:(c + 1) * FFN_CHUNK, :])
        ffn_acc.append(part + (ffn_acc[-1] if ffn_acc else x1buf[...]))

    def in_gate(base, i):
        return _sigmoid(_dot(h, w_in_ref[:, base + i * quarter:base + (i + 1) * quarter]))

    x = x_ref[0]
    gate_up(1)
    h = _rms(x, g_mix_ref[...]).astype(BF16)
    gate_up(2)
    a = _dot(h, w_in_ref[:, 0:O1])
    down(0)
    gate_up(3)
    z_v = _dot(h, w_in_ref[:, O1 + GMLP_WIDTH:O2])
    down(1)
    gate_up(4)
    pb = _pool_tile(a, b, t, jnp.where(t == 0, 0.0, halo_ref[...]), halo_ref, pool_ref)
    z_u = _dot(h, w_in_ref[:, O1:O1 + GMLP_WIDTH])
    down(2)
    gate_up(5)
    vb = _rms(jax.nn.gelu(z_v), g_v_ref[...]).astype(BF16)
    pa = _pool_project(pb, w_pool_ref, s_pool_ref)
    gate_a = [in_gate(O2, 0), in_gate(O2, 1)]
    down(3)
    gate_up(6)
    u = jax.nn.gelu(z_u)
    s_gate = _spatial_gate(vb, w_s_ref, bias_ref)
    gate_a += [in_gate(O2, 2), in_gate(O2, 3)]
    down(4)
    gate_up(7)
    sg = (u * s_gate).astype(BF16)
    gate_b = [in_gate(O3, 0), in_gate(O3, 1)]
    down(5)
    gate_up(8)
    gate_b += [in_gate(O3, 2), in_gate(O3, 3)]
    po_lo = _dot(pa, w_po_ref[:, 0:half])
    go_lo = _dot(sg, w_go_ref[:, 0:half])
    down(6)
    gate_up(9)
    merged_lo =(jnp.concatenate(gate_a[0:2], axis=1) * po_lo
                 + jnp.concatenate(gate_b[0:2], axis=1) * go_lo).astype(BF16)
    po_hi = _dot(pa, w_po_ref[:, half:])
    go_hi = _dot(sg, w_go_ref[:, half:])
    down(7)
    gate_up(10)
    merged_hi = (jnp.concatenate(gate_a[2:4], axis=1) * po_hi
                 + jnp.concatenate(gate_b[2:4], axis=1) * go_hi).astype(BF16)
    out_lo = _dot(merged_lo, w_out_ref[0:half, :])
    down(8)
    x1 = x + out_lo + _dot(merged_hi, w_out_ref[half:, :])
    down(9)
    x1buf[...] = x1
    h2_next = _rms(x1, g_ffn_ref[...]).astype(BF16)
    h2buf[...] = h2_next
    down(10)
    act0_next = _gate_up_chunk(h2_next, 0, w_gate_ref, w_up_ref)
    y_ref[0] = _rms(ffn_acc[-1], g_final_ref[...])
    act0buf[...] = act0_next


class _WeightStream:

    def __init__(self, jobs, stages, sems):
        self.stages = stages
        self.sems = sems
        self.plan = []
        self.job_end = []
        used = {c: 0 for c in stages}
        for src, dst in jobs:
            n_rows, c = dst.shape
            rows = stages[c].shape[1]
            for r0 in range(0, n_rows, rows):
                self.plan.append((src, dst, r0, min(rows, n_rows - r0), c,
                                  used[c] % STAGE_SLOTS))
                used[c] += 1
            self.job_end.append(len(self.plan))
        self.started = 0
        self.done = 0

    def _copy(self, i):
        src, _, r0, rr, c, slot = self.plan[i]
        return pltpu.make_async_copy(src.at[0, pl.ds(r0, rr), :],
                                     self.stages[c].at[slot, pl.ds(0, rr), :],
                                     self.sems.at[i % STAGE_SLOTS])

    def _start_through(self, last):
        while self.started <= min(last, len(self.plan) - 1):
            self._copy(self.started).start()
            self.started += 1

    def prime(self):
        self._start_through(STAGE_LOOKAHEAD - 1)

    def advance(self, count):
        for _ in range(count):
            i = self.done
            if i == len(self.plan):
                return
            self._start_through(i + STAGE_LOOKAHEAD)
            self._copy(i).wait()
            _, dst, r0, rr, c, slot = self.plan[i]
            dst[pl.ds(r0, rr), :] = self.stages[c][slot, 0:rr, :].astype(BF16)
            self.done += 1

    def advance_through_job(self, j):
        self.advance(self.job_end[j] - self.done)

    def finish(self):
        self.advance(len(self.plan) - self.done)


def _stage_rows(width):
    rows = STAGE_BYTES // (4 * width)
    return max(16, rows - rows % 16)


def _sample_rows(xs_ref, state_hbm, pool_s_hbm, w_in_ref, g_mix_ref, w_pool_ref,
                 s_pool_ref, w_s_ref, b_s_ref, g_v_ref, w_po_ref, w_go_ref,
                 w_out_ref, g_ffn_ref, w_gate_ref, w_up_ref, w_down_ref, g_final_ref,
                 ys_ref, vs_ref, load_state, tick, before_ffn, state_buf, a_buf, sems):
    shift_state = pltpu.make_async_copy(state_buf.at[pl.ds(1, POOL_STATE - 1)],
                                        pool_s_hbm.at[pl.ds(0, POOL_STATE - 1)],
                                        sems.at[1])
    store_row = pltpu.make_async_copy(a_buf, pool_s_hbm.at[POOL_STATE - 1], sems.at[2])

    x = xs_ref[...]
    h = _rms(x, g_mix_ref[...]).astype(BF16)
    a = _dot(h, w_in_ref[:, 0:O1])
    a_buf[...] = a
    store_row.start()
    tick()
    z = jax.nn.gelu(_dot(h, w_in_ref[:, O1:O2]))
    tick()
    gate_a = _sigmoid(_dot(h, w_in_ref[:, O2:O3]))
    tick()
    gate_b = _sigmoid(_dot(h, w_in_ref[:, O3:IN_WIDTH]))
    tick()

    load_state.wait()
    shift_state.start()
    parts = []
    for gi, w in enumerate(POOL_WINDOWS):
        lanes = slice(gi * POOL_GROUP, (gi + 1) * POOL_GROUP)
        xg = a[:, lanes]
        wsum = xg
        for k in range(1, w):
            wsum = wsum + state_buf[POOL_STATE - k, :, lanes]
        parts.append(wsum * (1.0 / w) - xg)
    pa = _pool_project(jnp.concatenate(parts, axis=1).astype(BF16), w_pool_ref, s_pool_ref)

    u = z[:, 0:GMLP_WIDTH]
    v = _rms(z[:, GMLP_WIDTH:], g_v_ref[...])
    vs_ref[...] = v
    w_diag = jnp.concatenate(
        [jnp.broadcast_to(w_s_ref[g, 0:1, 0:1], (1, GMLP_GROUP))
         for g in range(N_GMLP_GROUPS)], axis=1)
    bias0 = jnp.concatenate(
        [jnp.broadcast_to(b_s_ref[g:g + 1, 0:1], (1, GMLP_GROUP))
         for g in range(N_GMLP_GROUPS)], axis=1)
    sg = (u * (w_diag * v + bias0)).astype(BF16)

    tick()
    merged = (gate_a * _dot(pa, w_po_ref[...])
              + gate_b * _dot(sg, w_go_ref[...])).astype(BF16)
    tick()
    x1 = x + _dot(merged, w_out_ref[...])
    h2 = _rms(x1, g_ffn_ref[...]).astype(BF16)
    before_ffn()
    act = (_silu(_dot(h2, w_gate_ref[...])) * _dot(h2, w_up_ref[...])).astype(BF16)
    x2 = x1 + _dot(act, w_down_ref[...])
    ys_ref[...] = _rms(x2, g_final_ref[...])
    shift_state.wait()
    store_row.wait()


def _prepare_small(w_pool_f32_ref, b_s_ref, w_pool_ref, bias_ref):
    w_pool_ref[...] = jnp.zeros(w_pool_ref.shape, BF16)
    for g in range(len(POOL_WINDOWS)):
        lo = (g % 2) * POOL_GROUP
        w_pool_ref[g // 2, lo:lo + POOL_GROUP, lo:lo + POOL_GROUP] = (
            w_pool_f32_ref[g].astype(BF16))
    for g in range(N_GMLP_GROUPS):
        rows = jnp.broadcast_to(b_s_ref[g:g + 1, :], (GMLP_GROUP, CHUNK))
        bias_ref[:, g * GMLP_GROUP:(g + 1) * GMLP_GROUP] = jnp.transpose(rows)


def _layer_kernel(x_ref, xs_ref, state_hbm, w_in_hbm, w_po_hbm, w_go_hbm, w_out_hbm,
                  w_gate_hbm, w_up_hbm, w_down_hbm, g_mix_ref, w_pool_f32_ref,
                  s_pool_ref, w_s_ref, b_s_ref, g_v_ref, g_ffn_ref, g_final_ref,
                  y_ref, pool_ref, ys_ref, vs_ref, pool_s_hbm,
                  w_in_ref, w_po_ref, w_go_ref, w_out_ref, w_gate_ref, w_up_ref,
                  w_down_ref, w_pool_ref, bias_ref, halo_ref, x1buf, h2buf, act0buf,
                  *, n_tiles, tiles_per_seq):
    s = pl.program_id(0)
    mixer_refs = (x_ref, w_in_ref, g_mix_ref, w_pool_ref, s_pool_ref, w_s_ref, bias_ref,
                  g_v_ref, w_po_ref, w_go_ref, w_out_ref, g_ffn_ref)
    ffn_refs = (w_gate_ref, w_up_ref, w_down_ref, g_final_ref)
    carry = (halo_ref, x1buf, h2buf, act0buf)

    @pl.when(s == 0)
    def _():
        mixer_jobs = ((w_in_hbm, w_in_ref), (w_po_hbm, w_po_ref), (w_go_hbm, w_go_ref),
                      (w_out_hbm, w_out_ref))
        ffn_jobs = ((w_gate_hbm, w_gate_ref), (w_up_hbm, w_up_ref),
                    (w_down_hbm, w_down_ref))
        jobs = mixer_jobs + ffn_jobs
        widths = sorted({dst.shape[1] for _, dst in jobs})

        def first_step(*scoped):
            stages = dict(zip(widths, scoped[:len(widths)]))
            stream_sems, state_buf, a_buf, io_sems = scoped[len(widths):]
            stream = _WeightStream(jobs, stages, stream_sems)
            stream.prime()
            load_state = pltpu.make_async_copy(state_hbm, state_buf, io_sems.at[0])
            load_state.start()
            _prepare_small(w_pool_f32_ref, b_s_ref, w_pool_ref, bias_ref)
            stream.advance_through_job(len(mixer_jobs) - 1)
            tick = functools.partial(stream.advance, FIRST_STEP_TICK_CHUNKS)
            _mixer_first_step(*mixer_refs, pool_ref, halo_ref, x1buf, h2buf, tick)

            def before_ffn():
                stream.finish()
                act0buf[...] = _gate_up_chunk(h2buf[...], 0, w_gate_ref, w_up_ref)

            _sample_rows(xs_ref, state_hbm, pool_s_hbm, w_in_ref, g_mix_ref, w_pool_ref,
                         s_pool_ref, w_s_ref, b_s_ref, g_v_ref, w_po_ref, w_go_ref,
                         w_out_ref, g_ffn_ref, *ffn_refs, ys_ref, vs_ref, load_state,
                         tick, before_ffn, state_buf, a_buf, io_sems)

        pl.run_scoped(
            first_step,
            *[pltpu.VMEM((STAGE_SLOTS, _stage_rows(c), c), F32) for c in widths],
            pltpu.SemaphoreType.DMA((STAGE_SLOTS,)),
            pltpu.VMEM(state_hbm.shape, F32),
            pltpu.VMEM((state_hbm.shape[1], POOL_WIDTH), F32),
            pltpu.SemaphoreType.DMA((3,)))

    @pl.when(jnp.logical_and(s > 0, s < n_tiles))
    def _():
        _steady_step(lax.div(s, tiles_per_seq), lax.rem(s, tiles_per_seq),
                     *mixer_refs, *ffn_refs, y_ref, pool_ref, *carry)

    @pl.when(s == n_tiles)
    def _():
        _ffn_last_step(*ffn_refs, y_ref, x1buf, h2buf)


def _resident(shape):
    zeros = (0,) * len(shape)
    return pl.BlockSpec(shape, lambda s: zeros, pipeline_mode=pl.Buffered(1))


def kernel(x_prompt, x_sample, state_pool, w_in, g_mix, w_pool, s_pool, w_s, b_s, g_v,
           w_pool_out, w_gmlp_out, w_out, g_ffn, w_gate, w_up, w_down, g_final):
    batch, seq, _ = x_prompt.shape
    dec_batch = x_sample.shape[0]
    assert w_in.shape[0] == 1 and x_sample.shape[1] == 1
    assert seq % ROW_TILE == 0 and ROW_TILE % CHUNK == 0
    assert N_FFN_CHUNKS == 11

    g_mix2 = g_mix[0][None, :]
    s_pool2 = s_pool[0][None, :]
    g_v2 = g_v[0][None, :]
    g_ffn2 = g_ffn[0][None, :]
    g_final2 = g_final[None, :]
    state_t = jnp.transpose(state_pool[0], (1, 0, 2))

    tiles_per_seq = seq // ROW_TILE
    n_tiles = batch * tiles_per_seq

    def mixer_tile(s):
        i = jnp.minimum(s, n_tiles - 1)
        return i // tiles_per_seq, i % tiles_per_seq

    def ffn_tile(s):
        i = jnp.maximum(s - 1, 0)
        return i // tiles_per_seq, i % tiles_per_seq

    big_weights = (w_in, w_pool_out, w_gmlp_out, w_out, w_gate, w_up, w_down)
    small = (g_mix2, w_pool[0], s_pool2, w_s[0], b_s[0], g_v2, g_ffn2, g_final2)
    x_s = x_sample[:, 0, :]
    in_specs = [pl.BlockSpec((1, ROW_TILE, D_MODEL), lambda s: (*mixer_tile(s), 0)),
                _resident(x_s.shape)]
    in_specs += [pl.BlockSpec(memory_space=pl.ANY)] * (1 + len(big_weights))
    in_specs += [_resident(arr.shape) for arr in small]
    y_prompt, pool_p, y_s, v_s, pool_s = pl.pallas_call(
        functools.partial(_layer_kernel, n_tiles=n_tiles, tiles_per_seq=tiles_per_seq),
        grid=(n_tiles + 1,),
        in_specs=in_specs,
        out_specs=[
            pl.BlockSpec((1, ROW_TILE, D_MODEL), lambda s: (*ffn_tile(s), 0)),
            pl.BlockSpec((POOL_STATE, batch, POOL_WIDTH), lambda s: (0, 0, 0)),
            pl.BlockSpec((dec_batch, D_MODEL), lambda s: (0, 0)),
            pl.BlockSpec((dec_batch, GMLP_WIDTH), lambda s: (0, 0)),
            pl.BlockSpec(memory_space=pl.ANY),
        ],
        out_shape=[
            jax.ShapeDtypeStruct((batch, seq, D_MODEL), F32),
            jax.ShapeDtypeStruct((POOL_STATE, batch, POOL_WIDTH), F32),
            jax.ShapeDtypeStruct((dec_batch, D_MODEL), F32),
            jax.ShapeDtypeStruct((dec_batch, GMLP_WIDTH), F32),
            jax.ShapeDtypeStruct((POOL_STATE, dec_batch, POOL_WIDTH), F32),
        ],
        scratch_shapes=[pltpu.VMEM(w.shape[1:], BF16) for w in big_weights] + [
            pltpu.VMEM((2, 2 * POOL_GROUP, 2 * POOL_GROUP), BF16),
            pltpu.VMEM((CHUNK, GMLP_WIDTH), F32),
            pltpu.VMEM((HALO, POOL_WIDTH), F32),
            pltpu.VMEM((ROW_TILE, D_MODEL), F32),
            pltpu.VMEM((ROW_TILE, D_MODEL), BF16),
            pltpu.VMEM((ROW_TILE, FFN_CHUNK), BF16),
        ],
        compiler_params=pltpu.CompilerParams(
            dimension_semantics=("arbitrary",),
            vmem_limit_bytes=VMEM_LIMIT_BYTES),
        name="layer",
    )(x_prompt, x_s, state_t, *big_weights, *small)

    return (
        y_prompt,
        y_s[:, None, :],
        jnp.transpose(pool_p, (1, 0, 2))[None],
        jnp.transpose(pool_s, (1, 0, 2))[None],
        v_s[None, :, None, :],
    )
```

```python
import functools

import jax
import jax.numpy as jnp
from jax import lax
from jax.experimental import pallas as pl
from jax.experimental.pallas import tpu as pltpu

D_MODEL = 1024
POOL_WINDOWS = (2, 4, 8, 16)
POOL_WIDTH = 512
POOL_GROUP = 128
POOL_STATE = 15
GMLP_WIDTH = 512
N_GMLP_GROUPS = 4
GMLP_GROUP = 128
CHUNK = 128
D_FF = 2816
EPS = 1e-6

O1 = POOL_WIDTH
O2 = O1 + 2 * GMLP_WIDTH
O3 = O2 + D_MODEL
IN_WIDTH = O3 + D_MODEL

HALO = 16
ROW_TILE = 256
FFN_CHUNK = 256
N_FFN_CHUNKS = D_FF // FFN_CHUNK
VMEM_LIMIT_BYTES = 60 * 1024 * 1024

STAGE_SLOTS = 5
STAGE_LOOKAHEAD = STAGE_SLOTS - 1
STAGE_BYTES = 1 << 20
FIRST_STEP_TICK_CHUNKS = 4

F32 = jnp.float32
BF16 = jnp.bfloat16


def _dot(a, b):
    return jnp.dot(a, b, preferred_element_type=F32)


def _rms(x, g):
    ms = jnp.mean(x * x, axis=-1, keepdims=True)
    return x * lax.rsqrt(ms + EPS) * g


def _sigmoid(x):
    return 0.5 * jnp.tanh(0.5 * x) + 0.5


def _silu(x):
    h = 0.5 * x
    return h * jnp.tanh(h) + h


def _window_sums(ext, w):
    s = ext
    k = 1
    while k < w:
        s = s + pltpu.roll(s, k, axis=0)
        k *= 2
    return s


def _pool_tile(a, b, t, halo, halo_ref, pool_ref):
    tm = a.shape[0]
    ext = jnp.concatenate([halo, a], axis=0)
    pos = lax.broadcasted_iota(jnp.int32, (tm, 1), 0) + t * tm
    parts = []
    for gi, w in enumerate(POOL_WINDOWS):
        lanes = slice(gi * POOL_GROUP, (gi + 1) * POOL_GROUP)
        wsum = _window_sums(ext[:, lanes], w)[HALO:]
        inv_cnt = 1.0 / jnp.minimum(w, pos + 1).astype(F32)
        parts.append(wsum * inv_cnt - a[:, lanes])
    tail = a[tm - HALO:, :]
    for j in range(POOL_STATE):
        row = HALO - POOL_STATE + j
        pool_ref[j, pl.ds(b, 1), :] = tail[row:row + 1, :]
    halo_ref[...] = tail
    return jnp.concatenate(parts, axis=1).astype(BF16)


def _pool_project(pb, w_pool_ref, s_pool_ref):
    lo = _dot(pb[:, 0:256], w_pool_ref[0])
    hi = _dot(pb[:, 256:512], w_pool_ref[1])
    return (jnp.concatenate([lo, hi], axis=1) * s_pool_ref[...]).astype(BF16)


def _spatial_gate(vb, w_s_ref, bias_ref):
    causal = (lax.broadcasted_iota(jnp.int32, (CHUNK, CHUNK), 0)
              >= lax.broadcasted_iota(jnp.int32, (CHUNK, CHUNK), 1))
    w_causal = [jnp.where(causal, w_s_ref[g], 0.0).astype(BF16)
                for g in range(N_GMLP_GROUPS)]
    pair_w = 2 * GMLP_GROUP
    w_pairs = [jnp.concatenate(w_causal[2 * p:2 * p + 2], axis=1)
               for p in range(N_GMLP_GROUPS // 2)]
    first = lax.broadcasted_iota(jnp.int32, (CHUNK, pair_w), 1) < GMLP_GROUP
    no_v = jnp.zeros((CHUNK, pair_w), BF16)
    rows = []
    for c in range(vb.shape[0] // CHUNK):
        cols = []
        for p in range(N_GMLP_GROUPS // 2):
            v_pair = vb[c * CHUNK:(c + 1) * CHUNK, p * pair_w:(p + 1) * pair_w]
            v_diag = jnp.concatenate([jnp.where(first, v_pair, no_v),
                                      jnp.where(first, no_v, v_pair)], axis=0)
            cols.append(_dot(w_pairs[p], v_diag))
        rows.append(jnp.concatenate(cols, axis=1) + bias_ref[...])
    return jnp.concatenate(rows, axis=0)


def _gate_up_chunk(h2, c, w_gate_ref, w_up_ref):
    cols = slice(c * FFN_CHUNK, (c + 1) * FFN_CHUNK)
    return (_silu(_dot(h2, w_gate_ref[:, cols])) * _dot(h2, w_up_ref[:, cols])).astype(BF16)


def _mixer_first_step(x_ref, w_in_ref, g_mix_ref, w_pool_ref, s_pool_ref, w_s_ref,
                      bias_ref, g_v_ref, w_po_ref, w_go_ref, w_out_ref, g_ffn_ref,
                      pool_ref, halo_ref, x1buf, h2buf, tick, need_out_weights):
    x = x_ref[0]
    h = _rms(x, g_mix_ref[...]).astype(BF16)
    a = _dot(h, w_in_ref[:, 0:O1])
    tick()
    z_v = _dot(h, w_in_ref[:, O1 + GMLP_WIDTH:O2])
    tick()
    z_u = _dot(h, w_in_ref[:, O1:O1 + GMLP_WIDTH])
    tick()
    pb = _pool_tile(a, 0, 0, jnp.zeros((HALO, POOL_WIDTH), F32), halo_ref, pool_ref)
    gate_a = [_sigmoid(_dot(h, w_in_ref[:, O2 + i * 512:O2 + (i + 1) * 512]))
              for i in range(2)]
    tick()
    vb = _rms(jax.nn.gelu(z_v), g_v_ref[...]).astype(BF16)
    gate_b = [_sigmoid(_dot(h, w_in_ref[:, O3 + i * 512:O3 + (i + 1) * 512]))
              for i in range(2)]
    tick()
    pa = _pool_project(pb, w_pool_ref, s_pool_ref)
    sg = (jax.nn.gelu(z_u) * _spatial_gate(vb, w_s_ref, bias_ref)).astype(BF16)
    tick()
    need_out_weights()
    merged = (jnp.concatenate(gate_a, axis=1) * _dot(pa, w_po_ref[...])
              + jnp.concatenate(gate_b, axis=1) * _dot(sg, w_go_ref[...])).astype(BF16)
    tick()
    x1 = x + _dot(merged, w_out_ref[...])
    tick()
    x1buf[...] = x1
    h2buf[...] = _rms(x1, g_ffn_ref[...]).astype(BF16)


def _ffn_last_step(w_gate_ref, w_up_ref, w_down_ref, g_final_ref, y_ref, x1buf, h2buf):
    h2 = h2buf[...]
    act = (_silu(_dot(h2, w_gate_ref[...])) * _dot(h2, w_up_ref[...])).astype(BF16)
    x2 = x1buf[...] + _dot(act, w_down_ref[...])
    y_ref[0] = _rms(x2, g_final_ref[...])


def _steady_step(b, t, x_ref, w_in_ref, g_mix_ref, w_pool_ref, s_pool_ref, w_s_ref,
                 bias_ref, g_v_ref, w_po_ref, w_go_ref, w_out_ref, g_ffn_ref,
                 w_gate_ref, w_up_ref, w_down_ref, g_final_ref,
                 y_ref, pool_ref, halo_ref, x1buf, h2buf, act0buf):
    half = D_MODEL // 2
    quarter = D_MODEL // 4

    h2 = h2buf[...]
    acts = {0: act0buf[...]}
    ffn_acc = []

    def gate_up(c):
        acts[c] = _gate_up_chunk(h2, c, w_gate_ref, w_up_ref)

    def down(c):
        part = _dot(acts.pop(c), w_down_ref[c * FFN_CHUNK:(c + 1) * FFN_CHUNK, :])
        ffn_acc.append(part + (ffn_acc[-1] if ffn_acc else x1buf[...]))

    def in_gate(base, i):
        return _sigmoid(_dot(h, w_in_ref[:, base + i * quarter:base + (i + 1) * quarter]))

    x = x_ref[0]
    gate_up(1)
    h = _rms(x, g_mix_ref[...]).astype(BF16)
    gate_up(2)
    a = _dot(h, w_in_ref[:, 0:O1])
    down(0)
    gate_up(3)
    z_v = _dot(h, w_in_ref[:, O1 + GMLP_WIDTH:O2])
    down(1)
    gate_up(4)
    pb = _pool_tile(a, b, t, jnp.where(t == 0, 0.0, halo_ref[...]), halo_ref, pool_ref)
    z_u = _dot(h, w_in_ref[:, O1:O1 + GMLP_WIDTH])
    down(2)
    gate_up(5)
    vb = _rms(jax.nn.gelu(z_v), g_v_ref[...]).astype(BF16)
    pa = _pool_project(pb, w_pool_ref, s_pool_ref)
    gate_a = [in_gate(O2, 0), in_gate(O2, 1)]
    down(3)
    gate_up(6)
    u = jax.nn.gelu(z_u)
    s_gate = _spatial_gate(vb, w_s_ref, bias_ref)
    gate_a += [in_gate(O2, 2), in_gate(O2, 3)]
    down(4)
    gate_up(7)
    sg = (u * s_gate).astype(BF16)
    gate_b = [in_gate(O3, 0), in_gate(O3, 1)]
    down(5)
    gate_up(8)
    gate_b += [in_gate(O3, 2), in_gate(O3, 3)]
    po_lo = _dot(pa, w_po_ref[:, 0:half])
    go_lo = _dot(sg, w_go_ref[:, 0:half])
    down(6)
    gate_up(9)
    merged_lo =(jnp.concatenate(gate_a[0:2], axis=1) * po_lo
                 + jnp.concatenate(gate_b[0:2], axis=1) * go_lo).astype(BF16)
    po_hi = _dot(pa, w_po_ref[:, half:])
    go_hi = _dot(sg, w_go_ref[:, half:])
    down(7)
    gate_up(10)
    merged_hi = (jnp.concatenate(gate_a[2:4], axis=1) * po_hi
                 + jnp.concatenate(gate_b[2:4], axis=1) * go_hi).astype(BF16)
    out_lo = _dot(merged_lo, w_out_ref[0:half, :])
    down(8)
    x1 = x + out_lo + _dot(merged_hi, w_out_ref[half:, :])
    down(9)
    x1buf[...] = x1
    h2_next = _rms(x1, g_ffn_ref[...]).astype(BF16)
    h2buf[...] = h2_next
    down(10)
    act0_next = _gate_up_chunk(h2_next, 0, w_gate_ref, w_up_ref)
    y_ref[0] = _rms(ffn_acc[-1], g_final_ref[...])
    act0buf[...] = act0_next


class _WeightStream:

    def __init__(self, jobs, stages, sems):
        self.stages = stages
        self.sems = sems
        self.plan = []
        self.job_end = []
        used = {c: 0 for c in stages}
        for src, dst in jobs:
            n_rows, c = dst.shape
            rows = stages[c].shape[1]
            for r0 in range(0, n_rows, rows):
                self.plan.append((src, dst, r0, min(rows, n_rows - r0), c,
                                  used[c] % STAGE_SLOTS))
                used[c] += 1
            self.job_end.append(len(self.plan))
        self.started = 0
        self.done = 0

    def _copy(self, i):
        src, _, r0, rr, c, slot = self.plan[i]
        return pltpu.make_async_copy(src.at[0, pl.ds(r0, rr), :],
                                     self.stages[c].at[slot, pl.ds(0, rr), :],
                                     self.sems.at[i % STAGE_SLOTS])

    def _start_through(self, last):
        while self.started <= min(last, len(self.plan) - 1):
            self._copy(self.started).start()
            self.started += 1

    def prime(self):
        self._start_through(STAGE_LOOKAHEAD - 1)

    def advance(self, count):
        for _ in range(count):
            i = self.done
            if i == len(self.plan):
                return
            self._start_through(i + STAGE_LOOKAHEAD)
            self._copy(i).wait()
            _, dst, r0, rr, c, slot = self.plan[i]
            dst[pl.ds(r0, rr), :] = self.stages[c][slot, 0:rr, :].astype(BF16)
            self.done += 1

    def advance_through_job(self, j):
        self.advance(self.job_end[j] - self.done)

    def finish(self):
        self.advance(len(self.plan) - self.done)


def _stage_rows(width):
    rows = STAGE_BYTES // (4 * width)
    return max(16, rows - rows % 16)


def _sample_rows(xs_ref, state_hbm, pool_s_hbm, w_in_ref, g_mix_ref, w_pool_ref,
                 s_pool_ref, w_s_ref, b_s_ref, g_v_ref, w_po_ref, w_go_ref,
                 w_out_ref, g_ffn_ref, w_gate_ref, w_up_ref, w_down_ref, g_final_ref,
                 ys_ref, vs_ref, load_state, tick, before_ffn, state_buf, a_buf, sems):
    shift_state = pltpu.make_async_copy(state_buf.at[pl.ds(1, POOL_STATE - 1)],
                                        pool_s_hbm.at[pl.ds(0, POOL_STATE - 1)],
                                        sems.at[1])
    store_row = pltpu.make_async_copy(a_buf, pool_s_hbm.at[POOL_STATE - 1], sems.at[2])

    x = xs_ref[...]
    h = _rms(x, g_mix_ref[...]).astype(BF16)
    a = _dot(h, w_in_ref[:, 0:O1])
    a_buf[...] = a
    store_row.start()
    tick()
    z = jax.nn.gelu(_dot(h, w_in_ref[:, O1:O2]))
    tick()
    gate_a = _sigmoid(_dot(h, w_in_ref[:, O2:O3]))
    tick()
    gate_b = _sigmoid(_dot(h, w_in_ref[:, O3:IN_WIDTH]))
    tick()

    load_state.wait()
    shift_state.start()
    parts = []
    for gi, w in enumerate(POOL_WINDOWS):
        lanes = slice(gi * POOL_GROUP, (gi + 1) * POOL_GROUP)
        xg = a[:, lanes]
        wsum = xg
        for k in range(1, w):
            wsum = wsum + state_buf[POOL_STATE - k, :, lanes]
        parts.append(wsum * (1.0 / w) - xg)
    pa = _pool_project(jnp.concatenate(parts, axis=1).astype(BF16), w_pool_ref, s_pool_ref)

    u = z[:, 0:GMLP_WIDTH]
    v = _rms(z[:, GMLP_WIDTH:], g_v_ref[...])
    vs_ref[...] = v
    w_diag = jnp.concatenate(
        [jnp.broadcast_to(w_s_ref[g, 0:1, 0:1], (1, GMLP_GROUP))
         for g in range(N_GMLP_GROUPS)], axis=1)
    bias0 = jnp.concatenate(
        [jnp.broadcast_to(b_s_ref[g:g + 1, 0:1], (1, GMLP_GROUP))
         for g in range(N_GMLP_GROUPS)], axis=1)
    sg = (u * (w_diag * v + bias0)).astype(BF16)

    tick()
    merged = (gate_a * _dot(pa, w_po_ref[...])
              + gate_b * _dot(sg, w_go_ref[...])).astype(BF16)
    tick()
    x1 = x + _dot(merged, w_out_ref[...])
    h2 = _rms(x1, g_ffn_ref[...]).astype(BF16)
    before_ffn()
    act = (_silu(_dot(h2, w_gate_ref[...])) * _dot(h2, w_up_ref[...])).astype(BF16)
    x2 = x1 + _dot(act, w_down_ref[...])
    ys_ref[...] = _rms(x2, g_final_ref[...])
    shift_state.wait()
    store_row.wait()


def _prepare_small(w_pool_f32_ref, b_s_ref, w_pool_ref, bias_ref):
    w_pool_ref[...] = jnp.zeros(w_pool_ref.shape, BF16)
    for g in range(len(POOL_WINDOWS)):
        lo = (g % 2) * POOL_GROUP
        w_pool_ref[g // 2, lo:lo + POOL_GROUP, lo:lo + POOL_GROUP] = (
            w_pool_f32_ref[g].astype(BF16))
    for g in range(N_GMLP_GROUPS):
        rows = jnp.broadcast_to(b_s_ref[g:g + 1, :], (GMLP_GROUP, CHUNK))
        bias_ref[:, g * GMLP_GROUP:(g + 1) * GMLP_GROUP] = jnp.transpose(rows)


def _layer_kernel(x_ref, xs_ref, state_hbm, w_in_hbm, w_po_hbm, w_go_hbm, w_out_hbm,
                  w_gate_hbm, w_up_hbm, w_down_hbm, g_mix_ref, w_pool_f32_ref,
                  s_pool_ref, w_s_ref, b_s_ref, g_v_ref, g_ffn_ref, g_final_ref,
                  y_ref, pool_ref, ys_ref, vs_ref, pool_s_hbm,
                  w_in_ref, w_po_ref, w_go_ref, w_out_ref, w_gate_ref, w_up_ref,
                  w_down_ref, w_pool_ref, bias_ref, halo_ref, x1buf, h2buf, act0buf,
                  *, n_tiles, tiles_per_seq):
    s = pl.program_id(0)
    mixer_refs = (x_ref, w_in_ref, g_mix_ref, w_pool_ref, s_pool_ref, w_s_ref, bias_ref,
                  g_v_ref, w_po_ref, w_go_ref, w_out_ref, g_ffn_ref)
    ffn_refs = (w_gate_ref, w_up_ref, w_down_ref, g_final_ref)
    carry = (halo_ref, x1buf, h2buf, act0buf)

    @pl.when(s == 0)
    def _():
        mixer_jobs = ((w_in_hbm, w_in_ref), (w_po_hbm, w_po_ref), (w_go_hbm, w_go_ref),
                      (w_out_hbm, w_out_ref))
        ffn_jobs = ((w_gate_hbm, w_gate_ref), (w_up_hbm, w_up_ref),
                    (w_down_hbm, w_down_ref))
        jobs = mixer_jobs + ffn_jobs
        widths = sorted({dst.shape[1] for _, dst in jobs})

        def first_step(*scoped):
            stages = dict(zip(widths, scoped[:len(widths)]))
            stream_sems, state_buf, a_buf, io_sems = scoped[len(widths):]
            stream = _WeightStream(jobs, stages, stream_sems)
            stream.prime()
            load_state = pltpu.make_async_copy(state_hbm, state_buf, io_sems.at[0])
            load_state.start()
            _prepare_small(w_pool_f32_ref, b_s_ref, w_pool_ref, bias_ref)
            stream.advance_through_job(0)
            tick = functools.partial(stream.advance, FIRST_STEP_TICK_CHUNKS)
            need_out_weights = functools.partial(stream.advance_through_job,
                                                 len(mixer_jobs) - 1)
            _mixer_first_step(*mixer_refs, pool_ref, halo_ref, x1buf, h2buf, tick,
                              need_out_weights)

            def before_ffn():
                stream.finish()
                act0buf[...] = _gate_up_chunk(h2buf[...], 0, w_gate_ref, w_up_ref)

            _sample_rows(xs_ref, state_hbm, pool_s_hbm, w_in_ref, g_mix_ref, w_pool_ref,
                         s_pool_ref, w_s_ref, b_s_ref, g_v_ref, w_po_ref, w_go_ref,
                         w_out_ref, g_ffn_ref, *ffn_refs, ys_ref, vs_ref, load_state,
                         tick, before_ffn, state_buf, a_buf, io_sems)

        pl.run_scoped(
            first_step,
            *[pltpu.VMEM((STAGE_SLOTS, _stage_rows(c), c), F32) for c in widths],
            pltpu.SemaphoreType.DMA((STAGE_SLOTS,)),
            pltpu.VMEM(state_hbm.shape, F32),
            pltpu.VMEM((state_hbm.shape[1], POOL_WIDTH), F32),
            pltpu.SemaphoreType.DMA((3,)))

    @pl.when(jnp.logical_and(s > 0, s < n_tiles))
    def _():
        _steady_step(lax.div(s, tiles_per_seq), lax.rem(s, tiles_per_seq),
                     *mixer_refs, *ffn_refs, y_ref, pool_ref, *carry)

    @pl.when(s == n_tiles)
    def _():
        _ffn_last_step(*ffn_refs, y_ref, x1buf, h2buf)


def _resident(shape):
    zeros = (0,) * len(shape)
    return pl.BlockSpec(shape, lambda s: zeros, pipeline_mode=pl.Buffered(1))


def kernel(x_prompt, x_sample, state_pool, w_in, g_mix, w_pool, s_pool, w_s, b_s, g_v,
           w_pool_out, w_gmlp_out, w_out, g_ffn, w_gate, w_up, w_down, g_final):
    batch, seq, _ = x_prompt.shape
    dec_batch = x_sample.shape[0]
    assert w_in.shape[0] == 1 and x_sample.shape[1] == 1
    assert seq % ROW_TILE == 0 and ROW_TILE % CHUNK == 0
    assert N_FFN_CHUNKS == 11

    g_mix2 = g_mix[0][None, :]
    s_pool2 = s_pool[0][None, :]
    g_v2 = g_v[0][None, :]
    g_ffn2 = g_ffn[0][None, :]
    g_final2 = g_final[None, :]
    state_t = jnp.transpose(state_pool[0], (1, 0, 2))

    tiles_per_seq = seq // ROW_TILE
    n_tiles = batch * tiles_per_seq

    def mixer_tile(s):
        i = jnp.minimum(s, n_tiles - 1)
        return i // tiles_per_seq, i % tiles_per_seq

    def ffn_tile(s):
        i = jnp.maximum(s - 1, 0)
        return i // tiles_per_seq, i % tiles_per_seq

    big_weights = (w_in, w_pool_out, w_gmlp_out, w_out, w_gate, w_up, w_down)
    small = (g_mix2, w_pool[0], s_pool2, w_s[0], b_s[0], g_v2, g_ffn2, g_final2)
    x_s = x_sample[:, 0, :]
    in_specs = [pl.BlockSpec((1, ROW_TILE, D_MODEL), lambda s: (*mixer_tile(s), 0)),
                _resident(x_s.shape)]
    in_specs += [pl.BlockSpec(memory_space=pl.ANY)] * (1 + len(big_weights))
    in_specs += [_resident(arr.shape) for arr in small]
    y_prompt, pool_p, y_s, v_s, pool_s = pl.pallas_call(
        functools.partial(_layer_kernel, n_tiles=n_tiles, tiles_per_seq=tiles_per_seq),
        grid=(n_tiles + 1,),
        in_specs=in_specs,
        out_specs=[
            pl.BlockSpec((1, ROW_TILE, D_MODEL), lambda s: (*ffn_tile(s), 0)),
            pl.BlockSpec((POOL_STATE, batch, POOL_WIDTH), lambda s: (0, 0, 0)),
            pl.BlockSpec((dec_batch, D_MODEL), lambda s: (0, 0)),
            pl.BlockSpec((dec_batch, GMLP_WIDTH), lambda s: (0, 0)),
            pl.BlockSpec(memory_space=pl.ANY),
        ],
        out_shape=[
            jax.ShapeDtypeStruct((batch, seq, D_MODEL), F32),
            jax.ShapeDtypeStruct((POOL_STATE, batch, POOL_WIDTH), F32),
            jax.ShapeDtypeStruct((dec_batch, D_MODEL), F32),
            jax.ShapeDtypeStruct((dec_batch, GMLP_WIDTH), F32),
            jax.ShapeDtypeStruct((POOL_STATE, dec_batch, POOL_WIDTH), F32),
        ],
        scratch_shapes=[pltpu.VMEM(w.shape[1:], BF16) for w in big_weights] + [
            pltpu.VMEM((2, 2 * POOL_GROUP, 2 * POOL_GROUP), BF16),
            pltpu.VMEM((CHUNK, GMLP_WIDTH), F32),
            pltpu.VMEM((HALO, POOL_WIDTH), F32),
            pltpu.VMEM((ROW_TILE, D_MODEL), F32),
            pltpu.VMEM((ROW_TILE, D_MODEL), BF16),
            pltpu.VMEM((ROW_TILE, FFN_CHUNK), BF16),
        ],
        compiler_params=pltpu.CompilerParams(
            dimension_semantics=("arbitrary",),
            vmem_limit_bytes=VMEM_LIMIT_BYTES),
        name="layer",
    )(x_prompt, x_s, state_t, *big_weights, *small)

    return (
        y_prompt,
        y_s[:, None, :],
        jnp.transpose(pool_p, (1, 0, 2))[None],
        jnp.transpose(pool_s, (1, 0, 2))[None],
        v_s[None, :, None, :],
    )
```
